```python
import math
import jax, jax.numpy as jnp
from jax import lax
import numpy as np

D_MODEL = 1024
BATCH = 8
SEQ = 2048
DEPTH = 1
DEC_BATCH = 128
DEC_SEQ = 1
PAST_LEN = 16384
PAGE_SIZE = 128

SSD_EXPAND = 2
SSD_INNER = SSD_EXPAND * D_MODEL
SSD_HEADDIM = 64
SSD_HEADS = SSD_INNER // SSD_HEADDIM
SSD_GROUPS = 4
SSD_STATE = 128
SSD_CONV = 4
SSD_CHUNK = 128
SSD_XBC = SSD_INNER + 2 * SSD_GROUPS * SSD_STATE
CONF_DIM = D_MODEL
CONF_WIDTH = 31
MEM_TOKENS = 256
MEM_HEADS = 4
MEM_HEADDIM = 256
MEM_DIM = MEM_HEADS * MEM_HEADDIM
N_BRANCHES = 3
PEER_HEADS = 8
PEER_NKEYS = 128
PEER_EXPERTS = PEER_NKEYS * PEER_NKEYS
PEER_DKEY = 256
PEER_DHALF = PEER_DKEY // 2
PEER_TOPK = 16
PEER_BLOCK = 128
ALPHA = (2.0 * DEPTH) ** 0.25
BETA = (8.0 * DEPTH) ** -0.25
LN_EPS = 1e-5
S_Z = SSD_INNER
S_XBC = S_Z + SSD_XBC
S_DT = S_XBC + SSD_HEADS
S_CONF = S_DT + 2 * CONF_DIM
S_MEMQ = S_CONF + MEM_DIM
D_IN = S_MEMQ + N_BRANCHES * D_MODEL

kernel_name = 'hybrid_ssd_conformer_peer_decoder_step'


def _layer_norm(x, g, b):
    xf = x.astype(jnp.float32)
    mu = jnp.mean(xf, -1, keepdims=True)
    var = jnp.mean(jnp.square(xf - mu), -1, keepdims=True)
    return ((xf - mu) * lax.rsqrt(var + LN_EPS)).astype(x.dtype) * g + b


def _gated_rms_norm(y, z, w):
    h = (y * jax.nn.silu(z)).astype(jnp.float32)
    h = h * lax.rsqrt(jnp.mean(jnp.square(h), -1, keepdims=True) + LN_EPS)
    return h.astype(y.dtype) * w


def _causal_dwconv(u, buf, w, b):
    full = jnp.concatenate([buf.astype(u.dtype), u], axis=1)
    out = lax.conv_general_dilated(full, w.astype(u.dtype)[:, None, :], window_strides=(1,), padding='VALID',
                                   dimension_numbers=('NWC', 'WIO', 'NWC'), feature_group_count=u.shape[-1])
    return out + b, full[:, full.shape[1] - (w.shape[0] - 1):]


def _ssd_scan(x, dt, a, b_mat, c_mat, d_skip, h0):
    bsz, l, nh, p = x.shape
    g, n = b_mat.shape[2], b_mat.shape[3]
    r = nh // g
    q = min(SSD_CHUNK, l)
    nc = -(-l // q)
    pad = nc * q - l
    x_in = x
    if pad:
        padf = lambda t: jnp.pad(t, [(0, 0), (0, pad)] + [(0, 0)] * (t.ndim - 2))
        x, dt, b_mat, c_mat = padf(x), padf(dt), padf(b_mat), padf(c_mat)
    xc = x.reshape(bsz, nc, q, g, r, p)
    dtc = dt.reshape(bsz, nc, q, g, r)
    bc = b_mat.reshape(bsz, nc, q, g, n)
    cc = c_mat.reshape(bsz, nc, q, g, n)
    da = jnp.moveaxis(dtc.astype(jnp.float32) * a.astype(jnp.float32).reshape(g, r), 2, -1)
    a_cs = jnp.cumsum(da, axis=-1)
    xdt = xc * dtc[..., None]
    seg = a_cs[..., :, None] - a_cs[..., None, :]
    causal = jnp.tril(jnp.ones((q, q), dtype=bool))
    decay_ls = jnp.where(causal, jnp.exp(jnp.where(causal, seg, 0.0)), 0.0).astype(x.dtype)
    cb = jnp.einsum('bclgn,bcsgn->bcgls', cc, bc)
    y_diag = jnp.einsum('bcgls,bcgrls,bcsgrp->bclgrp', cb, decay_ls, xdt)
    decay_to_end = jnp.exp(a_cs[..., -1:] - a_cs).astype(x.dtype)
    states = jnp.einsum('bclgn,bcgrl,bclgrp->bcgrpn', bc, decay_to_end, xdt)
    chunk_decay = jnp.exp(a_cs[..., -1]).astype(states.dtype)

    def step(h, inp):
        st, dec = inp
        return h * dec[..., None, None] + st, h

    h_init = h0.reshape(bsz, g, r, p, n).astype(states.dtype)
    h_fin, h_prev = lax.scan(step, h_init, (jnp.moveaxis(states, 1, 0), jnp.moveaxis(chunk_decay, 1, 0)))
    h_prev = jnp.moveaxis(h_prev, 0, 1)
    y_off = jnp.einsum('bclgn,bcgrpn,bcgrl->bclgrp', cc, h_prev, jnp.exp(a_cs).astype(x.dtype))
    y = (y_diag + y_off).reshape(bsz, nc * q, nh, p)[:, :l] + x_in * d_skip[:, None]
    return y, h_fin.reshape(bsz, nh, p, n)


def _peer_block(t, w_q, sub_keys, u_tab, v_tab):
    nt = t.shape[0]
    qv = (t @ w_q).reshape(nt, PEER_HEADS, 2, PEER_DHALF)
    s = jnp.einsum('thkd,hknd->thkn', qv, sub_keys)
    s_top, i_top = lax.top_k(s, PEER_TOPK)
    cand = s_top[:, :, 0, :, None] + s_top[:, :, 1, None, :]
    cand_id = i_top[:, :, 0, :, None] * PEER_NKEYS + i_top[:, :, 1, None, :]
    best, pos = lax.top_k(cand.reshape(nt, PEER_HEADS, PEER_TOPK * PEER_TOPK), PEER_TOPK)
    idx = jnp.take_along_axis(cand_id.reshape(nt, PEER_HEADS, PEER_TOPK * PEER_TOPK), pos, axis=-1)
    gate = jax.nn.softmax(best.astype(jnp.float32), axis=-1).astype(t.dtype)
    u = jnp.take(u_tab, idx, axis=0)
    act = jax.nn.gelu(jnp.einsum('td,thkd->thk', t, u), approximate=False)
    v = jnp.take(v_tab, idx, axis=0)
    return jnp.einsum('thk,thkd->td', gate * act, v)


def _peer_ffn(x, w_q, sub_keys, u_tab, v_tab):
    shp = x.shape
    t = x.reshape(-1, shp[-1])
    n = t.shape[0]
    nb = -(-n // PEER_BLOCK)
    t = jnp.pad(t, ((0, nb * PEER_BLOCK - n), (0, 0))).reshape(nb, PEER_BLOCK, shp[-1])
    out = lax.map(lambda blk: _peer_block(blk, w_q, sub_keys, u_tab, v_tab), t)
    return out.reshape(nb * PEER_BLOCK, shp[-1])[:n].reshape(shp)


def _hybrid_layer(x, ssm_state, ssd_buf, conf_buf, mem_k, mem_v,
                  w_in, ssd_conv_w, ssd_conv_b, ssd_dt_bias, ssd_a_log, ssd_d, ssd_norm_w, ssd_w_out,
                  conf_conv_w, conf_conv_b, conf_ln_g, conf_ln_b, conf_w_out, mem_w_o, w_out, ln1_g, ln1_b,
                  peer_w_q, peer_sub_keys, peer_u, peer_v, ln2_g, ln2_b):
    b, l = x.shape[0], x.shape[1]
    proj = x @ w_in
    z, xbc, dt_raw, conf_in, q_mem, gates = jnp.split(proj, [S_Z, S_XBC, S_DT, S_CONF, S_MEMQ], axis=-1)
    xbc_c, new_ssd_buf = _causal_dwconv(xbc, ssd_buf, ssd_conv_w, ssd_conv_b)
    xbc_c = jax.nn.silu(xbc_c)
    xs, bm, cm = jnp.split(xbc_c, [SSD_INNER, SSD_INNER + SSD_GROUPS * SSD_STATE], axis=-1)
    dt = jax.nn.softplus(dt_raw + ssd_dt_bias)
    a = -jnp.exp(ssd_a_log)
    y_ssd, new_ssm = _ssd_scan(xs.reshape(b, l, SSD_HEADS, SSD_HEADDIM), dt, a,
                               bm.reshape(b, l, SSD_GROUPS, SSD_STATE), cm.reshape(b, l, SSD_GROUPS, SSD_STATE),
                               ssd_d, ssm_state)
    branch_a = _gated_rms_norm(y_ssd.reshape(b, l, SSD_INNER), z, ssd_norm_w) @ ssd_w_out
    glu = conf_in[..., :CONF_DIM] * jax.nn.sigmoid(conf_in[..., CONF_DIM:])
    cv, new_conf_buf = _causal_dwconv(glu, conf_buf, conf_conv_w, conf_conv_b)
    branch_b = jax.nn.silu(_layer_norm(cv, conf_ln_g, conf_ln_b)) @ conf_w_out
    qh = q_mem.reshape(b, l, MEM_HEADS, MEM_HEADDIM)
    sc = jnp.einsum('blhd,bmhd->bhlm', qh, mem_k).astype(jnp.float32) * (MEM_HEADDIM ** -0.5)
    pr = jax.nn.softmax(sc, axis=-1).astype(x.dtype)
    o = jnp.einsum('bhlm,bmhd->blhd', pr, mem_v).reshape(b, l, MEM_DIM)
    branch_c = o @ mem_w_o
    g = jax.nn.sigmoid(gates).reshape(b, l, N_BRANCHES, D_MODEL)
    merged = g[:, :, 0] * branch_a + g[:, :, 1] * branch_b + g[:, :, 2] * branch_c
    x = _layer_norm(ALPHA * x + merged @ w_out, ln1_g, ln1_b)
    x = _layer_norm(ALPHA * x + _peer_ffn(x, peer_w_q, peer_sub_keys, peer_u, peer_v), ln2_g, ln2_b)
    return x, new_ssm, new_ssd_buf, new_conf_buf


def setup_inputs(seed: int = 0) -> dict:
    key = jax.random.key(seed)
    ks = jax.random.split(key, 40)
    f32 = jnp.float32
    nrm = lambda k, shape, s: jax.random.normal(k, shape, f32) * s
    dt0 = jnp.exp(jax.random.uniform(ks[11], (DEPTH, SSD_HEADS), f32, math.log(1e-3), math.log(1e-1)))
    return {
        'x_prompt': nrm(ks[0], (BATCH, SEQ, D_MODEL), 1.0),
        'x_sample': nrm(ks[1], (DEC_BATCH, DEC_SEQ, D_MODEL), 1.0),
        'state_ssd': nrm(ks[2], (DEPTH, DEC_BATCH, SSD_HEADS, SSD_HEADDIM, SSD_STATE), 0.1),
        'state_ssd_conv': nrm(ks[3], (DEPTH, DEC_BATCH, SSD_CONV - 1, SSD_XBC), 1.0),
        'state_conf_conv': nrm(ks[4], (DEPTH, DEC_BATCH, CONF_WIDTH - 1, CONF_DIM), 0.5),
        'cache_mem_k': nrm(ks[5], (DEPTH, DEC_BATCH, MEM_TOKENS, MEM_HEADS, MEM_HEADDIM), 1.0),
        'cache_mem_v': nrm(ks[6], (DEPTH, DEC_BATCH, MEM_TOKENS, MEM_HEADS, MEM_HEADDIM), BETA),
        'mem_prompt': nrm(ks[7], (BATCH, MEM_TOKENS, D_MODEL), 1.0),
        'w_in': nrm(ks[8], (DEPTH, D_MODEL, D_IN), D_MODEL ** -0.5),
        'ssd_conv_w': nrm(ks[9], (DEPTH, SSD_CONV, SSD_XBC), SSD_CONV ** -0.5),
        'ssd_conv_b': nrm(ks[10], (DEPTH, SSD_XBC), 0.01),
        'ssd_dt_bias': dt0 + jnp.log(-jnp.expm1(-dt0)),
        'ssd_a_log': jnp.log(jax.random.uniform(ks[12], (DEPTH, SSD_HEADS), f32, 1.0, 16.0)),
        'ssd_d': 1.0 + nrm(ks[13], (DEPTH, SSD_HEADS), 0.1),
        'ssd_norm_w': 1.0 + nrm(ks[14], (DEPTH, SSD_INNER), 0.02),
        'ssd_w_out': nrm(ks[15], (DEPTH, SSD_INNER, D_MODEL), BETA * SSD_INNER ** -0.5),
        'conf_conv_w': nrm(ks[16], (DEPTH, CONF_WIDTH, CONF_DIM), CONF_WIDTH ** -0.5),
        'conf_conv_b': nrm(ks[17], (DEPTH, CONF_DIM), 0.01),
        'conf_ln_g': 1.0 + nrm(ks[18], (DEPTH, CONF_DIM), 0.02),
        'conf_ln_b': nrm(ks[19], (DEPTH, CONF_DIM), 0.01),
        'conf_w_out': nrm(ks[20], (DEPTH, CONF_DIM, D_MODEL), BETA * CONF_DIM ** -0.5),
        'mem_w_k': nrm(ks[21], (DEPTH, D_MODEL, MEM_DIM), D_MODEL ** -0.5),
        'mem_w_v': nrm(ks[22], (DEPTH, D_MODEL, MEM_DIM), BETA * D_MODEL ** -0.5),
        'mem_w_o': nrm(ks[23], (DEPTH, MEM_DIM, D_MODEL), BETA * MEM_DIM ** -0.5),
        'w_out': nrm(ks[24], (DEPTH, D_MODEL, D_MODEL), BETA * D_MODEL ** -0.5),
        'ln1_g': 1.0 + nrm(ks[25], (DEPTH, D_MODEL), 0.02),
        'ln1_b': nrm(ks[26], (DEPTH, D_MODEL), 0.01),
        'peer_w_q': nrm(ks[27], (DEPTH, D_MODEL, PEER_HEADS * PEER_DKEY), D_MODEL ** -0.5),
        'peer_sub_keys': nrm(ks[28], (DEPTH, PEER_HEADS, 2, PEER_NKEYS, PEER_DHALF), PEER_DHALF ** -0.5),
        'peer_u': nrm(ks[29], (DEPTH, PEER_EXPERTS, D_MODEL), D_MODEL ** -0.5),
        'peer_v': nrm(ks[30], (DEPTH, PEER_EXPERTS, D_MODEL), BETA * PEER_HEADS ** -0.5),
        'ln2_g': 1.0 + nrm(ks[31], (DEPTH, D_MODEL), 0.02),
        'ln2_b': nrm(ks[32], (DEPTH, D_MODEL), 0.01),
    }


def reference(x_prompt, x_sample, state_ssd, state_ssd_conv, state_conf_conv, cache_mem_k, cache_mem_v, mem_prompt,
              w_in, ssd_conv_w, ssd_conv_b, ssd_dt_bias, ssd_a_log, ssd_d, ssd_norm_w, ssd_w_out,
              conf_conv_w, conf_conv_b, conf_ln_g, conf_ln_b, conf_w_out, mem_w_k, mem_w_v, mem_w_o, w_out,
              ln1_g, ln1_b, peer_w_q, peer_sub_keys, peer_u, peer_v, ln2_g, ln2_b):
    bp = x_prompt.shape[0]
    yp, ys = x_prompt, x_sample
    p_ssm, p_sbuf, p_cbuf, p_mk, p_mv = [], [], [], [], []
    s_ssm, s_sbuf, s_cbuf = [], [], []
    for l in range(DEPTH):
        lw = (w_in[l], ssd_conv_w[l], ssd_conv_b[l], ssd_dt_bias[l], ssd_a_log[l], ssd_d[l], ssd_norm_w[l],
              ssd_w_out[l], conf_conv_w[l], conf_conv_b[l], conf_ln_g[l], conf_ln_b[l], conf_w_out[l],
              mem_w_o[l], w_out[l], ln1_g[l], ln1_b[l], peer_w_q[l], peer_sub_keys[l], peer_u[l], peer_v[l],
              ln2_g[l], ln2_b[l])
        mk = (mem_prompt @ mem_w_k[l]).reshape(bp, MEM_TOKENS, MEM_HEADS, MEM_HEADDIM)
        mv = (mem_prompt @ mem_w_v[l]).reshape(bp, MEM_TOKENS, MEM_HEADS, MEM_HEADDIM)
        h0 = jnp.zeros((bp, SSD_HEADS, SSD_HEADDIM, SSD_STATE), yp.dtype)
        sb0 = jnp.zeros((bp, SSD_CONV - 1, SSD_XBC), yp.dtype)
        cb0 = jnp.zeros((bp, CONF_WIDTH - 1, CONF_DIM), yp.dtype)
        yp, h_p, sb_p, cb_p = _hybrid_layer(yp, h0, sb0, cb0, mk, mv, *lw)
        p_ssm.append(h_p); p_sbuf.append(sb_p); p_cbuf.append(cb_p); p_mk.append(mk); p_mv.append(mv)
        ys, h_s, sb_s, cb_s = _hybrid_layer(ys, state_ssd[l], state_ssd_conv[l], state_conf_conv[l],
                                            cache_mem_k[l], cache_mem_v[l], *lw)
        s_ssm.append(h_s); s_sbuf.append(sb_s); s_cbuf.append(cb_s)
    new_ssd_p = jnp.stack(p_ssm)
    new_ssd_conv_p = jnp.stack(p_sbuf)
    new_conf_conv_p = jnp.stack(p_cbuf)
    new_mem_k_p = jnp.stack(p_mk)
    new_mem_v_p = jnp.stack(p_mv)
    new_ssd_s = jnp.stack(s_ssm)
    new_ssd_conv_s = jnp.stack(s_sbuf)
    new_conf_conv_s = jnp.stack(s_cbuf)
    return (yp, ys, new_ssd_p, new_ssd_conv_p, new_conf_conv_p, new_mem_k_p, new_mem_v_p,
            new_ssd_s, new_ssd_conv_s, new_conf_conv_s)
```

```python
import functools
import math

import jax
import jax.numpy as jnp
from jax import lax
from jax.experimental import pallas as pl
from jax.experimental.pallas import tpu as pltpu

F32 = jnp.float32
BF16 = jnp.bfloat16

D_MODEL = 1024
SSD_INNER = 2048
SSD_HEADDIM = 64
SSD_HEADS = 32
SSD_GROUPS = 4
SSD_STATE = 128
SSD_CONV = 4
SSD_CHUNK = 128
SSD_XBC = 3072
CONF_DIM = 1024
CONF_WIDTH = 31
MEM_TOKENS = 256
MEM_HEADS = 4
MEM_HEADDIM = 256
PEER_HEADS = 8
PEER_NKEYS = 128
PEER_EXPERTS = PEER_NKEYS * PEER_NKEYS
PEER_DHALF = 128
PEER_TOPK = 16
DEPTH = 1
ALPHA = (2.0 * DEPTH) ** 0.25
LN_EPS = 1e-5
S_Z = SSD_INNER
S_XBC = S_Z + SSD_XBC
S_DT = S_XBC + SSD_HEADS

COL_XBC = 2
COL_CONF_A = 5
COL_CONF_B = 6
COL_QMEM = 7
COL_GATES = 8
PROJ_COLS = 11 * 1024

LANES = 128
SUBLANES = 8
VMEM_LIMIT_BYTES = 56 * 1024 * 1024


def _params(*sem):
    return pltpu.CompilerParams(dimension_semantics=sem, vmem_limit_bytes=VMEM_LIMIT_BYTES)


def _sigmoid(x):
    return jax.nn.sigmoid(x)


def _silu(x):
    return x * _sigmoid(x)


def _gelu_erf(x):
    return 0.5 * x * (1.0 + lax.erf(x * (1.0 / math.sqrt(2.0))))


def _layer_norm_rows(x, g, b):
    mu = jnp.mean(x, axis=-1, keepdims=True)
    xc = x - mu
    var = jnp.mean(xc * xc, axis=-1, keepdims=True)
    return xc * lax.rsqrt(var + LN_EPS) * g + b


def _split3_bf16(x):
    hi = x.astype(BF16)
    r1 = x - hi.astype(F32)
    mid = r1.astype(BF16)
    lo = (r1 - mid.astype(F32)).astype(BF16)
    return hi, mid, lo


def _dot(a, b):
    return jnp.dot(a, b, preferred_element_type=F32)


def _dot_nt(a, b):
    return lax.dot_general(a, b, (((1,), (1,)), ((), ())), preferred_element_type=F32)


def _mm_kernel(x_ref, w_ref, o_ref):
    o_ref[...] = _dot(x_ref[...].astype(BF16), w_ref[...]).astype(o_ref.dtype)


def _matmul(x, w, tm, tn, name):
    m, k = x.shape
    n = w.shape[1]
    tm = min(tm, m)
    assert m % tm == 0 and n % tn == 0
    return pl.pallas_call(
        _mm_kernel,
        grid=(n // tn, m // tm),
        in_specs=[pl.BlockSpec((tm, k), lambda j, i: (i, 0)),
                  pl.BlockSpec((k, tn), lambda j, i: (0, j))],
        out_specs=pl.BlockSpec((tm, tn), lambda j, i: (i, j)),
        out_shape=jax.ShapeDtypeStruct((m, n), F32),
        compiler_params=_params("parallel", "parallel"),
        name=name,
    )(x, w)


SSD_CONV_ROWS = 512
CONV_CHUNK = 32


def _ssd_conv_kernel(x_ref, halo_ref, w_ref, b_ref, o_ref, scr):
    tl = x_ref.shape[0]
    first = pl.program_id(1) == 0
    scr[0:SUBLANES, :] = jnp.where(first, 0.0, halo_ref[...])
    scr[SUBLANES:SUBLANES + tl, :] = x_ref[...]
    for r0 in range(0, tl, CONV_CHUNK):
        acc = jnp.broadcast_to(b_ref[...], (CONV_CHUNK, x_ref.shape[1]))
        for k in range(SSD_CONV):
            acc = acc + w_ref[k:k + 1, :] * scr[pl.ds(r0 + SUBLANES - (SSD_CONV - 1) + k, CONV_CHUNK), :]
        o_ref[r0:r0 + CONV_CHUNK, :] = _silu(acc)


def _ssd_conv_prompt(proj, w, b, nb, seq):
    tl = SSD_CONV_ROWS
    nl = seq // tl
    rows_per_halo = tl // SUBLANES
    return pl.pallas_call(
        _ssd_conv_kernel,
        grid=(nb, nl, SSD_XBC // 1024),
        in_specs=[
            pl.BlockSpec((tl, 1024), lambda bi, l, j: (bi * nl + l, COL_XBC + j)),
            pl.BlockSpec((SUBLANES, 1024),
                         lambda bi, l, j: (jnp.maximum((bi * nl + l) * rows_per_halo - 1, 0), COL_XBC + j)),
            pl.BlockSpec((SSD_CONV, 1024), lambda bi, l, j: (0, j)),
            pl.BlockSpec((1, 1024), lambda bi, l, j: (0, j)),
        ],
        out_specs=pl.BlockSpec((tl, 1024), lambda bi, l, j: (bi * nl + l, j)),
        out_shape=jax.ShapeDtypeStruct((nb * seq, SSD_XBC), F32),
        scratch_shapes=[pltpu.VMEM((SUBLANES + tl, 1024), F32)],
        compiler_params=_params("parallel", "parallel", "parallel"),
        name="ssd_conv_prompt",
    )(proj, proj, w, b)


def _softplus(x):
    return jnp.maximum(x, 0.0) + jnp.log1p(jnp.exp(-jnp.abs(x)))


def _ssd_scan_kernel(xc_ref, z_ref, dtraw_ref, dtb_ref, alog_ref, d_ref, nw_ref,
                     y_ref, st_ref, yscr):
    q = SSD_CHUNK
    c = pl.program_id(1)

    @pl.when(c == 0)
    def _():
        st_ref[...] = jnp.zeros_like(st_ref)

    dt = _softplus(dtraw_ref[...] + dtb_ref[...])
    a = -jnp.exp(alog_ref[...])
    da = dt * a
    row = lax.broadcasted_iota(jnp.int32, (q, q), 0)
    col = lax.broadcasted_iota(jnp.int32, (q, q), 1)
    causal = row >= col
    tri = jnp.where(causal, 1.0, 0.0).astype(BF16)
    hi, mid, lo = _split3_bf16(da)
    a_cs = _dot(tri, hi) + _dot(tri, mid) + _dot(tri, lo)
    a_cs_t = a_cs.T
    dt_t = dt.T
    xs = xc_ref[:, 0:SSD_INNER]
    xs_t = xs.T
    for g in range(SSD_GROUPS):
        b_g = xc_ref[:, SSD_INNER + g * SSD_STATE:SSD_INNER + (g + 1) * SSD_STATE]
        c_g = xc_ref[:, SSD_INNER + (SSD_GROUPS + g) * SSD_STATE:SSD_INNER + (SSD_GROUPS + g + 1) * SSD_STATE]
        b_g16 = b_g.astype(BF16)
        cb = _dot_nt(c_g.astype(BF16), b_g16)
        heads_per_group = SSD_HEADS // SSD_GROUPS
        for r in range(0, heads_per_group, 2):
            pair = []
            for h in (g * heads_per_group + r, g * heads_per_group + r + 1):
                acs_col = a_cs[:, h:h + 1]
                acs_row = a_cs_t[h:h + 1, :]
                dt_row = dt_t[h:h + 1, :]
                seg = jnp.where(causal, acs_col - acs_row, -1e30)
                m1 = (cb * jnp.exp(seg) * dt_row).astype(BF16)
                m2 = (c_g * jnp.exp(acs_col)).astype(BF16)
                x_h = xs[:, h * SSD_HEADDIM:(h + 1) * SSD_HEADDIM].astype(BF16)
                h_prev = st_ref[0, h]
                y_h = _dot(m1, x_h) + _dot_nt(m2, h_prev.astype(BF16))
                pair.append(y_h)
                last = a_cs_t[h:h + 1, q - 1:q]
                w_row = jnp.exp(last - acs_row) * dt_row
                xw_t = (xs_t[h * SSD_HEADDIM:(h + 1) * SSD_HEADDIM, :] * w_row).astype(BF16)
                st_ref[0, h] = h_prev * jnp.exp(last) + _dot(xw_t, b_g16)
            h0 = g * heads_per_group + r
            yscr[:, h0 * SSD_HEADDIM:(h0 + 2) * SSD_HEADDIM] = jnp.concatenate(pair, axis=1)
    y = yscr[...] + xs * d_ref[...]
    hgate = y * _silu(z_ref[...])
    ms = jnp.mean(hgate * hgate, axis=-1, keepdims=True)
    y_ref[...] = (hgate * lax.rsqrt(ms + LN_EPS) * nw_ref[...]).astype(y_ref.dtype)


def _ssd_scan_prompt(xbc_c, proj, dt_raw, dt_bias, a_log, d_row, norm_w, nb, seq):
    q = SSD_CHUNK
    nc = seq // q
    return pl.pallas_call(
        _ssd_scan_kernel,
        grid=(nb, nc),
        in_specs=[
            pl.BlockSpec((q, SSD_XBC), lambda bi, c: (bi * nc + c, 0)),
            pl.BlockSpec((q, SSD_INNER), lambda bi, c: (bi * nc + c, 0)),
            pl.BlockSpec((q, LANES), lambda bi, c: (bi * nc + c, 0)),
            pl.BlockSpec((1, LANES), lambda bi, c: (0, 0)),
            pl.BlockSpec((1, LANES), lambda bi, c: (0, 0)),
            pl.BlockSpec((1, SSD_INNER), lambda bi, c: (0, 0)),
            pl.BlockSpec((1, SSD_INNER), lambda bi, c: (0, 0)),
        ],
        out_specs=[
            pl.BlockSpec((q, SSD_INNER), lambda bi, c: (bi * nc + c, 0)),
            pl.BlockSpec((1, SSD_HEADS, SSD_HEADDIM, SSD_STATE), lambda bi, c: (bi, 0, 0, 0)),
        ],
        out_shape=[jax.ShapeDtypeStruct((nb * seq, SSD_INNER), BF16),
                   jax.ShapeDtypeStruct((nb, SSD_HEADS, SSD_HEADDIM, SSD_STATE), F32)],
        scratch_shapes=[pltpu.VMEM((q, SSD_INNER), F32)],
        compiler_params=_params("parallel", "arbitrary"),
        name="ssd_scan_prompt",
    )(xbc_c, proj, dt_raw, dt_bias, a_log, d_row, norm_w)


def _ssd_conv_step_kernel(x_ref, st_ref, w_ref, b_ref, o_ref):
    acc = b_ref[...] + w_ref[SSD_CONV - 1:SSD_CONV, :] * x_ref[...]
    for k in range(SSD_CONV - 1):
        acc = acc + w_ref[k:k + 1, :] * st_ref[k]
    o_ref[...] = _silu(acc)


def _ssd_conv_step(proj, state_t, w, b, nb):
    return pl.pallas_call(
        _ssd_conv_step_kernel,
        grid=(SSD_XBC // 1024,),
        in_specs=[
            pl.BlockSpec((nb, 1024), lambda j: (0, COL_XBC + j)),
            pl.BlockSpec((SSD_CONV - 1, nb, 1024), lambda j: (0, 0, j)),
            pl.BlockSpec((SSD_CONV, 1024), lambda j: (0, j)),
            pl.BlockSpec((1, 1024), lambda j: (0, j)),
        ],
        out_specs=pl.BlockSpec((nb, 1024), lambda j: (0, j)),
        out_shape=jax.ShapeDtypeStruct((nb, SSD_XBC), F32),
        compiler_params=_params("parallel"),
        name="ssd_conv_step",
    )(proj, state_t, w, b)


def _ssd_step_kernel(xc_ref, bc_ref, dtraw_ref, dtb_ref, alog_ref, st_ref, st_out_ref, yt_ref, xdt_scr, dec_scr):
    nb = xc_ref.shape[0]
    bi = pl.program_id(0)
    rows = SSD_HEADS * SSD_HEADDIM

    @pl.when(bi == 0)
    def _():
        dt = _softplus(dtraw_ref[...] + dtb_ref[...])
        a = -jnp.exp(alog_ref[...])
        dt_t = dt.T
        da_t = (dt * a).T
        expand = lambda t: jnp.concatenate(
            [jnp.broadcast_to(t[h:h + 1, :], (SSD_HEADDIM, nb)) for h in range(SSD_HEADS)], axis=0)
        xs_t = xc_ref[:, 0:SSD_INNER].T
        xdt_scr[...] = xs_t * expand(dt_t)
        dec_scr[...] = jnp.exp(expand(da_t))
        yt_ref[...] = jnp.zeros_like(yt_ref)

    lane_b = lax.broadcasted_iota(jnp.int32, (nb, LANES), 0)
    onehot = jnp.where(lane_b == bi, 1.0, 0.0).astype(BF16)

    def pick(ref):
        hi, mid, lo = _split3_bf16(ref[...])
        return _dot(hi, onehot) + _dot(mid, onehot) + _dot(lo, onehot)

    xdt_b = pick(xdt_scr)
    dec_b = pick(dec_scr)
    out_lane = lax.broadcasted_iota(jnp.int32, (rows // SSD_GROUPS, LANES), 1) == bi
    for g in range(SSD_GROUPS):
        r0 = g * (rows // SSD_GROUPS)
        r1 = r0 + rows // SSD_GROUPS
        b_row = bc_ref[0, :, g * SSD_STATE:(g + 1) * SSD_STATE]
        c_row = bc_ref[0, :, (SSD_GROUPS + g) * SSD_STATE:(SSD_GROUPS + g + 1) * SSD_STATE]
        h_new = st_ref[0, r0:r1, :] * dec_b[r0:r1, :] + xdt_b[r0:r1, :] * b_row
        st_out_ref[0, r0:r1, :] = h_new
        y_col = jnp.sum(h_new * c_row, axis=-1, keepdims=True)
        yt_ref[r0:r1, :] = jnp.where(out_lane, y_col, yt_ref[r0:r1, :])


def _ssd_step(xbc_c, dt_raw, dt_bias, a_log, state, nb):
    rows = SSD_HEADS * SSD_HEADDIM
    assert nb == LANES
    bc_rows = xbc_c[:, SSD_INNER:].reshape(nb, 1, SSD_XBC - SSD_INNER)
    return pl.pallas_call(
        _ssd_step_kernel,
        grid=(nb,),
        in_specs=[
            pl.BlockSpec((nb, SSD_XBC), lambda bi: (0, 0)),
            pl.BlockSpec((1, 1, bc_rows.shape[2]), lambda bi: (bi, 0, 0)),
            pl.BlockSpec((nb, LANES), lambda bi: (0, 0)),
            pl.BlockSpec((1, LANES), lambda bi: (0, 0)),
            pl.BlockSpec((1, LANES), lambda bi: (0, 0)),
            pl.BlockSpec((1, rows, SSD_STATE), lambda bi: (bi, 0, 0)),
        ],
        out_specs=[
            pl.BlockSpec((1, rows, SSD_STATE), lambda bi: (bi, 0, 0)),
            pl.BlockSpec((rows, nb), lambda bi: (0, 0)),
        ],
        out_shape=[jax.ShapeDtypeStruct((nb, rows, SSD_STATE), F32),
                   jax.ShapeDtypeStruct((rows, nb), F32)],
        scratch_shapes=[pltpu.VMEM((rows, nb), F32), pltpu.VMEM((rows, nb), F32)],
        compiler_params=_params("arbitrary"),
        name="ssd_step",
    )(xbc_c, bc_rows, dt_raw, dt_bias, a_log, state)


def _gated_norm_step_kernel(yt_ref, xc_ref, z_ref, d_ref, nw_ref, o_ref):
    y = yt_ref[...].T + xc_ref[:, 0:SSD_INNER] * d_ref[...]
    hgate = y * _silu(z_ref[...])
    ms = jnp.mean(hgate * hgate, axis=-1, keepdims=True)
    o_ref[...] = (hgate * lax.rsqrt(ms + LN_EPS) * nw_ref[...]).astype(o_ref.dtype)


def _gated_norm_step(y_t, xbc_c, proj, d_row, norm_w, nb):
    return pl.pallas_call(
        _gated_norm_step_kernel,
        grid=(1,),
        in_specs=[
            pl.BlockSpec((SSD_INNER, nb), lambda i: (0, 0)),
            pl.BlockSpec((nb, SSD_XBC), lambda i: (0, 0)),
            pl.BlockSpec((nb, SSD_INNER), lambda i: (0, 0)),
            pl.BlockSpec((1, SSD_INNER), lambda i: (0, 0)),
            pl.BlockSpec((1, SSD_INNER), lambda i: (0, 0)),
        ],
        out_specs=pl.BlockSpec((nb, SSD_INNER), lambda i: (0, 0)),
        out_shape=jax.ShapeDtypeStruct((nb, SSD_INNER), BF16),
        compiler_params=_params("arbitrary"),
        name="ssd_gated_norm_step",
    )(y_t, xbc_c, proj, d_row, norm_w)


CONF_ROWS = 256
CONF_PAD = 32
CONF_W_ROWS = 32


def _conf_kernel(a_ref, b_ref, w_ref, cb_ref, g_ref, beta_ref, o_ref, st_ref, gl, cv):
    tl = a_ref.shape[0]
    l = pl.program_id(1)

    @pl.when(l == 0)
    def _():
        gl[0:CONF_PAD, :] = jnp.zeros((CONF_PAD, CONF_DIM), F32)

    gl[CONF_PAD:CONF_PAD + tl, :] = a_ref[...] * _sigmoid(b_ref[...])
    for r0 in range(0, tl, CONV_CHUNK):
        acc = jnp.broadcast_to(cb_ref[...], (CONV_CHUNK, CONF_DIM))
        for k in range(CONF_WIDTH):
            acc = acc + w_ref[k:k + 1, :] * gl[pl.ds(r0 + CONF_PAD - (CONF_WIDTH - 1) + k, CONV_CHUNK), :]
        cv[r0:r0 + CONV_CHUNK, :] = acc
    o_ref[...] = _silu(_layer_norm_rows(cv[...], g_ref[...], beta_ref[...])).astype(o_ref.dtype)

    @pl.when(l == pl.num_programs(1) - 1)
    def _():
        st_ref[0] = gl[CONF_PAD + tl - (CONF_WIDTH - 1):CONF_PAD + tl, :]

    gl[0:CONF_PAD, :] = gl[tl:tl + CONF_PAD, :]


def _conf_prompt(proj, w, cb, ln_g, ln_b, nb, seq):
    tl = CONF_ROWS
    nl = seq // tl
    vec = pl.BlockSpec((1, CONF_DIM), lambda bi, l: (0, 0))
    return pl.pallas_call(
        _conf_kernel,
        grid=(nb, nl),
        in_specs=[
            pl.BlockSpec((tl, 1024), lambda bi, l: (bi * nl + l, COL_CONF_A)),
            pl.BlockSpec((tl, 1024), lambda bi, l: (bi * nl + l, COL_CONF_B)),
            pl.BlockSpec((CONF_W_ROWS, CONF_DIM), lambda bi, l: (0, 0)),
            vec, vec, vec,
        ],
        out_specs=[
            pl.BlockSpec((tl, CONF_DIM), lambda bi, l: (bi * nl + l, 0)),
            pl.BlockSpec((1, CONF_WIDTH - 1, CONF_DIM), lambda bi, l: (bi, 0, 0)),
        ],
        out_shape=[jax.ShapeDtypeStruct((nb * seq, CONF_DIM), BF16),
                   jax.ShapeDtypeStruct((nb, CONF_WIDTH - 1, CONF_DIM), F32)],
        scratch_shapes=[pltpu.VMEM((CONF_PAD + tl, CONF_DIM), F32), pltpu.VMEM((tl, CONF_DIM), F32)],
        compiler_params=_params("parallel", "arbitrary"),
        name="conf_prompt",
    )(proj, proj, w, cb, ln_g, ln_b)


def _conf_step_kernel(a_ref, b_ref, st_ref, w_ref, cb_ref, g_ref, beta_ref, o_ref, glu_ref):
    glu = a_ref[...] * _sigmoid(b_ref[...])
    glu_ref[...] = glu
    acc = cb_ref[...] + w_ref[CONF_WIDTH - 1:CONF_WIDTH, :] * glu
    for k in range(CONF_WIDTH - 1):
        acc = acc + w_ref[k:k + 1, :] * st_ref[k]
    o_ref[...] = _silu(_layer_norm_rows(acc, g_ref[...], beta_ref[...])).astype(o_ref.dtype)


def _conf_step(proj, state_t, w, cb, ln_g, ln_b, nb):
    vec = pl.BlockSpec((1, CONF_DIM), lambda i: (0, 0))
    return pl.pallas_call(
        _conf_step_kernel,
        grid=(1,),
        in_specs=[
            pl.BlockSpec((nb, 1024), lambda i: (0, COL_CONF_A)),
            pl.BlockSpec((nb, 1024), lambda i: (0, COL_CONF_B)),
            pl.BlockSpec((CONF_WIDTH - 1, nb, CONF_DIM), lambda i: (0, 0, 0)),
            pl.BlockSpec((CONF_W_ROWS, CONF_DIM), lambda i: (0, 0)),
            vec, vec, vec,
        ],
        out_specs=[pl.BlockSpec((nb, CONF_DIM), lambda i: (0, 0)),
                   pl.BlockSpec((nb, CONF_DIM), lambda i: (0, 0))],
        out_shape=[jax.ShapeDtypeStruct((nb, CONF_DIM), BF16),
                   jax.ShapeDtypeStruct((nb, CONF_DIM), F32)],
        compiler_params=_params("arbitrary"),
        name="conf_step",
    )(proj, proj, state_t, w, cb, ln_g, ln_b)


ATTN_ROWS = 512
MEM_SCALE = MEM_HEADDIM ** -0.5


def _softmax_lanes(sc):
    mx = jnp.max(sc, axis=-1, keepdims=True)
    e = jnp.exp(sc - mx)
    return e / jnp.sum(e, axis=-1, keepdims=True)


def _attn_kernel(q_ref, k_ref, v_ref, o_ref):
    for h in range(MEM_HEADS):
        sl = slice(h * MEM_HEADDIM, (h + 1) * MEM_HEADDIM)
        sc = _dot_nt(q_ref[:, sl].astype(BF16), k_ref[:, sl].astype(BF16)) * MEM_SCALE
        pr = _softmax_lanes(sc).astype(BF16)
        o_ref[:, sl] = _dot(pr, v_ref[:, sl].astype(BF16)).astype(o_ref.dtype)


def _attn_prompt(proj, mk, mv, nb, seq):
    tq = ATTN_ROWS
    nq = seq // tq
    kv = pl.BlockSpec((MEM_TOKENS, 1024), lambda bi, i: (bi, 0))
    return pl.pallas_call(
        _attn_kernel,
        grid=(nb, nq),
        in_specs=[pl.BlockSpec((tq, 1024), lambda bi, i: (bi * nq + i, COL_QMEM)), kv, kv],
        out_specs=pl.BlockSpec((tq, 1024), lambda bi, i: (bi * nq + i, 0)),
        out_shape=jax.ShapeDtypeStruct((nb * seq, 1024), BF16),
        compiler_params=_params("parallel", "parallel"),
        name="attn_prompt",
    )(proj, mk, mv)


ATTN_STEP_BATCH = 8


def _attn_step_kernel(q_ref, k_ref, v_ref, o_ref):
    head_of_lane = lax.broadcasted_iota(jnp.int32, (SUBLANES, 1024), 1) // MEM_HEADDIM
    own = head_of_lane == lax.broadcasted_iota(jnp.int32, (SUBLANES, 1024), 0)
    for i in range(ATTN_STEP_BATCH):
        q_row = q_ref[i:i + 1, :]
        q_bd = jnp.where(own, q_row, 0.0).astype(BF16)
        sc = _dot_nt(q_bd, k_ref[i].astype(BF16)) * MEM_SCALE
        pr = _softmax_lanes(sc).astype(BF16)
        o_all = _dot(pr, v_ref[i].astype(BF16))
        o_ref[i:i + 1, :] = jnp.sum(jnp.where(own, o_all, 0.0), axis=0, keepdims=True).astype(o_ref.dtype)


def _attn_step(proj, mem_k, mem_v, nb):
    bb = ATTN_STEP_BATCH
    kv = pl.BlockSpec((bb, MEM_TOKENS, 1024), lambda i: (i, 0, 0))
    return pl.pallas_call(
        _attn_step_kernel,
        grid=(nb // bb,),
        in_specs=[pl.BlockSpec((bb, 1024), lambda i: (i, COL_QMEM)), kv, kv],
        out_specs=pl.BlockSpec((bb, 1024), lambda i: (i, 0)),
        out_shape=jax.ShapeDtypeStruct((nb, 1024), F32),
        compiler_params=_params("parallel"),
        name="attn_step",
    )(proj, mem_k, mem_v)


def _merge_kernel(yn_ref, ca_ref, at_ref, ga_ref, gb_ref, gc_ref, x_ref, wa_ref, wb_ref, wc_ref, wo_ref,
                  g_ref, b_ref, o_ref):
    br_a = _dot(yn_ref[...].astype(BF16), wa_ref[...])
    br_b = _dot(ca_ref[...].astype(BF16), wb_ref[...])
    br_c = _dot(at_ref[...].astype(BF16), wc_ref[...])
    merged = _sigmoid(ga_ref[...]) * br_a + _sigmoid(gb_ref[...]) * br_b + _sigmoid(gc_ref[...]) * br_c
    res = ALPHA * x_ref[...] + _dot(merged.astype(BF16), wo_ref[...])
    o_ref[...] = _layer_norm_rows(res, g_ref[...], b_ref[...])


def _merge(ynorm, cact, attn, proj, x, w_a, w_b, w_c, w_o, ln_g, ln_b, tm):
    t = x.shape[0]
    row = lambda w: pl.BlockSpec((tm, w), lambda i: (i, 0))
    gate = lambda j: pl.BlockSpec((tm, D_MODEL), lambda i: (i, COL_GATES + j))
    full = lambda a: pl.BlockSpec(a.shape, lambda i: (0, 0))
    return pl.pallas_call(
        _merge_kernel,
        grid=(t // tm,),
        in_specs=[row(SSD_INNER), row(CONF_DIM), row(1024), gate(0), gate(1), gate(2),
                  row(D_MODEL), full(w_a), full(w_b), full(w_c), full(w_o), full(ln_g), full(ln_b)],
        out_specs=row(D_MODEL),
        out_shape=jax.ShapeDtypeStruct((t, D_MODEL), F32),
        compiler_params=_params("parallel"),
        name="merge_ln1",
    )(ynorm, cact, attn, proj, proj, proj, x, w_a, w_b, w_c, w_o, ln_g, ln_b)


N_SUBKEYS = 2 * PEER_HEADS
STAIRCASE = [(i, j) for i in range(PEER_TOPK) for j in range(PEER_TOPK) if (i + 1) * (j + 1) <= PEER_TOPK]


def _compare_exchange(v, i, j):
    a, b = v[i], v[j]
    v[i] = jnp.maximum(a, b)
    v[j] = jnp.minimum(a, b)


def _bitonic_merge_desc(v):
    n = len(v)
    j = n // 2
    while j >= 1:
        for i in range(n):
            if i ^ j > i:
                _compare_exchange(v, i, i ^ j)
        j //= 2


def _bitonic_sort_desc(v):
    n = len(v)
    k = 2
    while k <= n:
        j = k // 2
        while j >= 1:
            for i in range(n):
                l = i ^ j
                if l > i:
                    if i & k == 0:
                        _compare_exchange(v, i, l)
                    else:
                        _compare_exchange(v, l, i)
            j //= 2
        k *= 2


def _top16_sorted(s_t):
    v = [s_t[r * SUBLANES:(r + 1) * SUBLANES, :] for r in range(PEER_NKEYS // SUBLANES)]
    _bitonic_sort_desc(v)
    for shift in (4, 2, 1):
        other = [pltpu.roll(x, shift, 0) for x in v]
        v = [jnp.maximum(v[i], other[PEER_TOPK - 1 - i]) for i in range(PEER_TOPK)]
        _bitonic_merge_desc(v)
    return v


def _peer_score_kernel(x_ref, wq_ref, keys_ref, s_ref, e_ref, tau_ref, qv_scr, top_scr):
    tm = x_ref.shape[0]
    qv_scr[...] = _dot(x_ref[...].astype(BF16), wq_ref[...]).astype(BF16)
    for hk in range(N_SUBKEYS):
        q_hk = qv_scr[:, hk * PEER_DHALF:(hk + 1) * PEER_DHALF]
        s_t = _dot_nt(keys_ref[hk], q_hk)
        s_ref[hk] = s_t
        top = _top16_sorted(s_t)
        for i in range(PEER_TOPK):
            top_scr[hk % 2, i, hk // 2:hk // 2 + 1, :] = top[i][0:1, :]
    a = [top_scr[0, i] for i in range(PEER_TOPK)]
    b = [top_scr[1, i] for i in range(PEER_TOPK)]
    cand = [a[i] + b[j] for i, j in STAIRCASE]
    tau = jnp.full((PEER_HEADS, tm), -jnp.inf, F32)
    for c in cand:
        cnt = jnp.zeros((PEER_HEADS, tm), F32)
        for c2 in cand:
            cnt = cnt + jnp.where(c2 >= c, 1.0, 0.0)
        tau = jnp.maximum(tau, jnp.where(cnt >= float(PEER_TOPK), c, -jnp.inf))
    top_sum = a[0] + b[0]
    z = jnp.zeros((PEER_HEADS, tm), F32)
    for c in cand:
        z = z + jnp.where(c >= tau, jnp.exp(c - top_sum), 0.0)
    tau_ref[...] = tau
    inv_z = 1.0 / z
    for h in range(PEER_HEADS):
        e_ref[2 * h] = jnp.exp(s_ref[2 * h] - a[0][h:h + 1, :])
        e_ref[2 * h + 1] = jnp.exp(s_ref[2 * h + 1] - b[0][h:h + 1, :]) * inv_z[h:h + 1, :]


def _peer_scores(x, w_q, keys, tm):
    t = x.shape[0]
    se = pl.BlockSpec((N_SUBKEYS, PEER_NKEYS, tm), lambda i: (0, 0, i))
    return pl.pallas_call(
        _peer_score_kernel,
        grid=(t // tm,),
        in_specs=[pl.BlockSpec((tm, D_MODEL), lambda i: (i, 0)),
                  pl.BlockSpec(w_q.shape, lambda i: (0, 0)),
                  pl.BlockSpec(keys.shape, lambda i: (0, 0, 0))],
        out_specs=[se, se, pl.BlockSpec((PEER_HEADS, tm), lambda i: (0, i))],
        out_shape=[jax.ShapeDtypeStruct((N_SUBKEYS, PEER_NKEYS, t), F32),
                   jax.ShapeDtypeStruct((N_SUBKEYS, PEER_NKEYS, t), F32),
                   jax.ShapeDtypeStruct((PEER_HEADS, t), F32)],
        scratch_shapes=[pltpu.VMEM((tm, N_SUBKEYS * PEER_DHALF), BF16),
                        pltpu.VMEM((2, PEER_TOPK, PEER_HEADS, tm), F32)],
        compiler_params=_params("parallel"),
        name="peer_scores",
    )(x, w_q, keys)


PEER_EXPERT_BLOCK = 1024
I_PER_BLOCK = PEER_EXPERT_BLOCK // PEER_NKEYS
assert I_PER_BLOCK == SUBLANES


def _peer_expert_kernel(x_ref, u_ref, vt_ref, s_ref, e_ref, tau_ref, g_ref, b_ref, o_ref,
                        acc_t, act_t, h_t, xb):
    tm = x_ref.shape[0]
    eb = pl.program_id(1)

    @pl.when(eb == 0)
    def _():
        acc_t[...] = jnp.zeros_like(acc_t)
        xb[...] = x_ref[...].astype(BF16)

    act_t[...] = _dot_nt(u_ref[...], xb[...])

    def per_first_key(ii, carry):
        g0 = pl.multiple_of(eb * I_PER_BLOCK, SUBLANES)
        to_row0 = lax.rem(SUBLANES - ii, SUBLANES)
        r0 = pl.multiple_of(ii * PEER_NKEYS, PEER_NKEYS)
        for c0 in range(0, tm, LANES):
            cols = slice(c0, c0 + LANES)
            gate = jnp.zeros((PEER_NKEYS, LANES), F32)
            for h in range(PEER_HEADS):
                s0 = pltpu.roll(s_ref[2 * h, pl.ds(g0, SUBLANES), cols], to_row0, 0)[0:1, :]
                e0 = pltpu.roll(e_ref[2 * h, pl.ds(g0, SUBLANES), cols], to_row0, 0)[0:1, :]
                sel = (s0 + s_ref[2 * h + 1, :, cols]) >= tau_ref[h:h + 1, cols]
                gate = gate + jnp.where(sel, e0 * e_ref[2 * h + 1, :, cols], 0.0)
            act = _gelu_erf(act_t[pl.ds(r0, PEER_NKEYS), cols])
            h_t[pl.ds(r0, PEER_NKEYS), cols] = (gate * act).astype(BF16)
        return carry

    lax.fori_loop(0, I_PER_BLOCK, per_first_key, 0)
    acc_t[...] += _dot(vt_ref[...], h_t[...])

    @pl.when(eb == pl.num_programs(1) - 1)
    def _():
        res = ALPHA * x_ref[...] + acc_t[...].T
        o_ref[...] = _layer_norm_rows(res, g_ref[...], b_ref[...])


def _peer_experts(x, u16, vt16, s_t, e_t, tau, ln_g, ln_b, tm):
    t = x.shape[0]
    te = PEER_EXPERT_BLOCK
    se = pl.BlockSpec((N_SUBKEYS, PEER_NKEYS, tm), lambda i, e: (0, 0, i))
    vec = pl.BlockSpec((1, D_MODEL), lambda i, e: (0, 0))
    return pl.pallas_call(
        _peer_expert_kernel,
        grid=(t // tm, PEER_EXPERTS // te),
        in_specs=[pl.BlockSpec((tm, D_MODEL), lambda i, e: (i, 0)),
                  pl.BlockSpec((te, D_MODEL), lambda i, e: (e, 0)),
                  pl.BlockSpec((D_MODEL, te), lambda i, e: (0, e)),
                  se, se,
                  pl.BlockSpec((PEER_HEADS, tm), lambda i, e: (0, i)),
                  vec, vec],
        out_specs=pl.BlockSpec((tm, D_MODEL), lambda i, e: (i, 0)),
        out_shape=jax.ShapeDtypeStruct((t, D_MODEL), F32),
        scratch_shapes=[pltpu.VMEM((D_MODEL, tm), F32), pltpu.VMEM((te, tm), F32),
                        pltpu.VMEM((te, tm), BF16), pltpu.VMEM((tm, D_MODEL), BF16)],
        compiler_params=_params("parallel", "arbitrary"),
        name="peer_experts",
    )(x, u16, vt16, s_t, e_t, tau, ln_g, ln_b)


def _pad_lanes(v):
    return jnp.pad(v, (0, LANES - v.shape[0])).reshape(1, LANES)


def _prepare_weights(w_in, ssd_conv_w, ssd_conv_b, ssd_dt_bias, ssd_a_log, ssd_d, ssd_norm_w, ssd_w_out,
                     conf_conv_w, conf_conv_b, conf_ln_g, conf_ln_b, conf_w_out, mem_w_k, mem_w_v, mem_w_o,
                     w_out, ln1_g, ln1_b, peer_w_q, peer_sub_keys, peer_u, peer_v, ln2_g, ln2_b):
    w = w_in[0]
    row = lambda v: v[0].reshape(1, -1)
    return dict(
        w_main=jnp.concatenate([w[:, :S_XBC], w[:, S_DT:]], axis=1).astype(BF16),
        w_dt=jnp.pad(w[:, S_XBC:S_DT], ((0, 0), (0, LANES - SSD_HEADS))).astype(BF16),
        ssd_conv_w=ssd_conv_w[0], ssd_conv_b=row(ssd_conv_b),
        dt_bias=_pad_lanes(ssd_dt_bias[0]), a_log=_pad_lanes(ssd_a_log[0]),
        d_row=jnp.repeat(ssd_d[0], SSD_HEADDIM).reshape(1, SSD_INNER), norm_w=row(ssd_norm_w),
        ssd_w_out=ssd_w_out[0].astype(BF16),
        conf_w=jnp.pad(conf_conv_w[0], ((0, CONF_W_ROWS - CONF_WIDTH), (0, 0))), conf_b=row(conf_conv_b),
        conf_ln_g=row(conf_ln_g), conf_ln_b=row(conf_ln_b), conf_w_out=conf_w_out[0].astype(BF16),
        mem_w_k=mem_w_k[0].astype(BF16), mem_w_v=mem_w_v[0].astype(BF16), mem_w_o=mem_w_o[0].astype(BF16),
        w_out=w_out[0].astype(BF16), ln1_g=row(ln1_g), ln1_b=row(ln1_b),
        peer_w_q=peer_w_q[0].astype(BF16),
        peer_keys=peer_sub_keys[0].reshape(N_SUBKEYS, PEER_NKEYS, PEER_DHALF).astype(BF16),
        peer_u=peer_u[0].astype(BF16), peer_vt=peer_v[0].T.astype(BF16),
        ln2_g=row(ln2_g), ln2_b=row(ln2_b),
    )


def _tokenwise_tail(p, x, proj, ynorm, cact, attn, tm_merge, tm_score, tm_expert):
    x1 = _merge(ynorm, cact, attn, proj, x, p["ssd_w_out"], p["conf_w_out"], p["mem_w_o"], p["w_out"],
                p["ln1_g"], p["ln1_b"], tm_merge)
    s_t, e_t, tau = _peer_scores(x1, p["peer_w_q"], p["peer_keys"], tm_score)
    return _peer_experts(x1, p["peer_u"], p["peer_vt"], s_t, e_t, tau, p["ln2_g"], p["ln2_b"], tm_expert)


def _prompt_layer(p, x_prompt, mem_prompt):
    nb, seq, _ = x_prompt.shape
    x = x_prompt.reshape(nb * seq, D_MODEL)
    proj = _matmul(x, p["w_main"], 1024, 1024, "in_proj_prompt")
    dt_raw = _matmul(x, p["w_dt"], 1024, LANES, "dt_proj_prompt")
    mem = mem_prompt.reshape(nb * MEM_TOKENS, D_MODEL)
    mk = _matmul(mem, p["mem_w_k"], 1024, 1024, "mem_k_proj")
    mv = _matmul(mem, p["mem_w_v"], 1024, 1024, "mem_v_proj")
    xbc_c = _ssd_conv_prompt(proj, p["ssd_conv_w"], p["ssd_conv_b"], nb, seq)
    ynorm, ssm = _ssd_scan_prompt(xbc_c, proj, dt_raw, p["dt_bias"], p["a_log"], p["d_row"], p["norm_w"], nb, seq)
    cact, conf_state = _conf_prompt(proj, p["conf_w"], p["conf_b"], p["conf_ln_g"], p["conf_ln_b"], nb, seq)
    attn = _attn_prompt(proj, mk, mv, nb, seq)
    y = _tokenwise_tail(p, x, proj, ynorm, cact, attn, 256, 256, 512)
    ssd_buf = proj.reshape(nb, seq, PROJ_COLS)[:, seq - (SSD_CONV - 1):, S_Z:S_XBC]
    kv_shape = (nb, MEM_TOKENS, MEM_HEADS, MEM_HEADDIM)
    return (y.reshape(nb, seq, D_MODEL), ssm, ssd_buf, conf_state, mk.reshape(kv_shape), mv.reshape(kv_shape))


def _sample_layer(p, x_sample, state_ssd, state_ssd_conv, state_conf_conv, cache_mem_k, cache_mem_v):
    nb = x_sample.shape[0]
    x = x_sample.reshape(nb, D_MODEL)
    proj = _matmul(x, p["w_main"], nb, 1024, "in_proj_sample")
    dt_raw = _matmul(x, p["w_dt"], nb, LANES, "dt_proj_sample")
    xbc_c = _ssd_conv_step(proj, jnp.swapaxes(state_ssd_conv, 0, 1), p["ssd_conv_w"], p["ssd_conv_b"], nb)
    rows = SSD_HEADS * SSD_HEADDIM
    ssm, y_t = _ssd_step(xbc_c, dt_raw, p["dt_bias"], p["a_log"], state_ssd.reshape(nb, rows, SSD_STATE), nb)
    ynorm = _gated_norm_step(y_t, xbc_c, proj, p["d_row"], p["norm_w"], nb)
    cact, glu = _conf_step(proj, jnp.swapaxes(state_conf_conv, 0, 1), p["conf_w"], p["conf_b"],
                           p["conf_ln_g"], p["conf_ln_b"], nb)
    attn = _attn_step(proj, cache_mem_k.reshape(nb, MEM_TOKENS, 1024), cache_mem_v.reshape(nb, MEM_TOKENS, 1024), nb)
    y = _tokenwise_tail(p, x, proj, ynorm, cact, attn, nb, nb, nb)
    ssd_buf = jnp.concatenate([state_ssd_conv[:, 1:], proj[:, None, S_Z:S_XBC]], axis=1)
    conf_buf = jnp.concatenate([state_conf_conv[:, 1:], glu[:, None, :]], axis=1)
    return (y.reshape(nb, 1, D_MODEL), ssm.reshape(nb, SSD_HEADS, SSD_HEADDIM, SSD_STATE), ssd_buf, conf_buf)


def kernel(x_prompt, x_sample, state_ssd, state_ssd_conv, state_conf_conv, cache_mem_k, cache_mem_v, mem_prompt, w_in, ssd_conv_w, ssd_conv_b, ssd_dt_bias, ssd_a_log, ssd_d, ssd_norm_w, ssd_w_out, conf_conv_w, conf_conv_b, conf_ln_g, conf_ln_b, conf_w_out, mem_w_k, mem_w_v, mem_w_o, w_out, ln1_g, ln1_b, peer_w_q, peer_sub_keys, peer_u, peer_v, ln2_g, ln2_b):
    assert w_in.shape[0] == DEPTH == 1
    p = _prepare_weights(w_in, ssd_conv_w, ssd_conv_b, ssd_dt_bias, ssd_a_log, ssd_d, ssd_norm_w, ssd_w_out,
                         conf_conv_w, conf_conv_b, conf_ln_g, conf_ln_b, conf_w_out, mem_w_k, mem_w_v, mem_w_o,
                         w_out, ln1_g, ln1_b, peer_w_q, peer_sub_keys, peer_u, peer_v, ln2_g, ln2_b)
    yp, ssm_p, sbuf_p, cbuf_p, mk_p, mv_p = _prompt_layer(p, x_prompt, mem_prompt)
    ys, ssm_s, sbuf_s, cbuf_s = _sample_layer(p, x_sample, state_ssd[0], state_ssd_conv[0], state_conf_conv[0],
                                              cache_mem_k[0], cache_mem_v[0])
    return (yp, ys, ssm_p[None], sbuf_p[None], cbuf_p[None], mk_p[None], mv_p[None],
            ssm_s[None], sbuf_s[None], cbuf_s[None])
```

```python
import functools
import math

import jax
import jax.numpy as jnp
from jax import lax
from jax.experimental import pallas as pl
from jax.experimental.pallas import tpu as pltpu

F32 = jnp.float32
BF16 = jnp.bfloat16

D_MODEL = 1024
SSD_INNER = 2048
SSD_HEADDIM = 64
SSD_HEADS = 32
SSD_GROUPS = 4
SSD_STATE = 128
SSD_CONV = 4
SSD_CHUNK = 128
SSD_XBC = 3072
CONF_DIM = 1024
CONF_WIDTH = 31
MEM_TOKENS = 256
MEM_HEADS = 4
MEM_HEADDIM = 256
PEER_HEADS = 8
PEER_NKEYS = 128
PEER_EXPERTS = PEER_NKEYS * PEER_NKEYS
PEER_DHALF = 128
PEER_TOPK = 16
DEPTH = 1
ALPHA = (2.0 * DEPTH) ** 0.25
LN_EPS = 1e-5
S_Z = SSD_INNER
S_XBC = S_Z + SSD_XBC
S_DT = S_XBC + SSD_HEADS

COL_XBC = 2
COL_CONF_A = 5
COL_CONF_B = 6
COL_QMEM = 7
COL_GATES = 8
PROJ_COLS = 11 * 1024

LANES = 128
SUBLANES = 8
VMEM_LIMIT_BYTES = 56 * 1024 * 1024


def _params(*sem):
    return pltpu.CompilerParams(dimension_semantics=sem, vmem_limit_bytes=VMEM_LIMIT_BYTES)


def _sigmoid(x):
    return jax.nn.sigmoid(x)


def _silu(x):
    return x * _sigmoid(x)


def _gelu_erf(x):
    return 0.5 * x * (1.0 + lax.erf(x * (1.0 / math.sqrt(2.0))))


def _layer_norm_rows(x, g, b):
    mu = jnp.mean(x, axis=-1, keepdims=True)
    xc = x - mu
    var = jnp.mean(xc * xc, axis=-1, keepdims=True)
    return xc * lax.rsqrt(var + LN_EPS) * g + b


def _split3_bf16(x):
    hi = x.astype(BF16)
    r1 = x - hi.astype(F32)
    mid = r1.astype(BF16)
    lo = (r1 - mid.astype(F32)).astype(BF16)
    return hi, mid, lo


def _dot(a, b):
    return jnp.dot(a, b, preferred_element_type=F32)


def _dot_nt(a, b):
    return lax.dot_general(a, b, (((1,), (1,)), ((), ())), preferred_element_type=F32)


def _mm_kernel(x_ref, w_ref, o_ref):
    o_ref[...] = _dot(x_ref[...].astype(BF16), w_ref[...]).astype(o_ref.dtype)


def _matmul(x, w, tm, tn, name):
    m, k = x.shape
    n = w.shape[1]
    tm = min(tm, m)
    assert m % tm == 0 and n % tn == 0
    return pl.pallas_call(
        _mm_kernel,
        grid=(n // tn, m // tm),
        in_specs=[pl.BlockSpec((tm, k), lambda j, i: (i, 0)),
                  pl.BlockSpec((k, tn), lambda j, i: (0, j))],
        out_specs=pl.BlockSpec((tm, tn), lambda j, i: (i, j)),
        out_shape=jax.ShapeDtypeStruct((m, n), F32),
        compiler_params=_params("parallel", "parallel"),
        name=name,
    )(x, w)


SSD_CONV_ROWS = 512
CONV_CHUNK = 32


def _ssd_conv_kernel(x_ref, halo_ref, w_ref, b_ref, o_ref, scr):
    tl = x_ref.shape[0]
    first = pl.program_id(1) == 0
    scr[0:SUBLANES, :] = jnp.where(first, 0.0, halo_ref[...])
    scr[SUBLANES:SUBLANES + tl, :] = x_ref[...]
    for r0 in range(0, tl, CONV_CHUNK):
        acc = jnp.broadcast_to(b_ref[...], (CONV_CHUNK, x_ref.shape[1]))
        for k in range(SSD_CONV):
            acc = acc + w_ref[k:k + 1, :] * scr[pl.ds(r0 + SUBLANES - (SSD_CONV - 1) + k, CONV_CHUNK), :]
        o_ref[r0:r0 + CONV_CHUNK, :] = _silu(acc)


def _ssd_conv_prompt(proj, w, b, nb, seq):
    tl = SSD_CONV_ROWS
    nl = seq // tl
    rows_per_halo = tl // SUBLANES
    return pl.pallas_call(
        _ssd_conv_kernel,
        grid=(nb, nl, SSD_XBC // 1024),
        in_specs=[
            pl.BlockSpec((tl, 1024), lambda bi, l, j: (bi * nl + l, COL_XBC + j)),
            pl.BlockSpec((SUBLANES, 1024),
                         lambda bi, l, j: (jnp.maximum((bi * nl + l) * rows_per_halo - 1, 0), COL_XBC + j)),
            pl.BlockSpec((SSD_CONV, 1024), lambda bi, l, j: (0, j)),
            pl.BlockSpec((1, 1024), lambda bi, l, j: (0, j)),
        ],
        out_specs=pl.BlockSpec((tl, 1024), lambda bi, l, j: (bi * nl + l, j)),
        out_shape=jax.ShapeDtypeStruct((nb * seq, SSD_XBC), F32),
        scratch_shapes=[pltpu.VMEM((SUBLANES + tl, 1024), F32)],
        compiler_params=_params("parallel", "parallel", "parallel"),
        name="ssd_conv_prompt",
    )(proj, proj, w, b)


def _softplus(x):
    return jnp.maximum(x, 0.0) + jnp.log1p(jnp.exp(-jnp.abs(x)))


def _ssd_scan_kernel(xc_ref, z_ref, dtraw_ref, dtb_ref, alog_ref, d_ref, nw_ref,
                     y_ref, st_ref, yscr):
    q = SSD_CHUNK
    c = pl.program_id(1)

    @pl.when(c == 0)
    def _():
        st_ref[...] = jnp.zeros_like(st_ref)

    dt = _softplus(dtraw_ref[...] + dtb_ref[...])
    a = -jnp.exp(alog_ref[...])
    da = dt * a
    row = lax.broadcasted_iota(jnp.int32, (q, q), 0)
    col = lax.broadcasted_iota(jnp.int32, (q, q), 1)
    causal = row >= col
    tri = jnp.where(causal, 1.0, 0.0).astype(BF16)
    hi, mid, lo = _split3_bf16(da)
    a_cs = _dot(tri, hi) + _dot(tri, mid) + _dot(tri, lo)
    a_cs_t = a_cs.T
    dt_t = dt.T
    xs = xc_ref[:, 0:SSD_INNER]
    xs_t = xs.T
    for g in range(SSD_GROUPS):
        b_g = xc_ref[:, SSD_INNER + g * SSD_STATE:SSD_INNER + (g + 1) * SSD_STATE]
        c_g = xc_ref[:, SSD_INNER + (SSD_GROUPS + g) * SSD_STATE:SSD_INNER + (SSD_GROUPS + g + 1) * SSD_STATE]
        b_g16 = b_g.astype(BF16)
        cb = _dot_nt(c_g.astype(BF16), b_g16)
        heads_per_group = SSD_HEADS // SSD_GROUPS
        for r in range(0, heads_per_group, 2):
            pair = []
            for h in (g * heads_per_group + r, g * heads_per_group + r + 1):
                acs_col = a_cs[:, h:h + 1]
                acs_row = a_cs_t[h:h + 1, :]
                dt_row = dt_t[h:h + 1, :]
                seg = jnp.where(causal, acs_col - acs_row, -1e30)
                m1 = (cb * jnp.exp(seg) * dt_row).astype(BF16)
                m2 = (c_g * jnp.exp(acs_col)).astype(BF16)
                x_h = xs[:, h * SSD_HEADDIM:(h + 1) * SSD_HEADDIM].astype(BF16)
                h_prev = st_ref[0, h]
                y_h = _dot(m1, x_h) + _dot_nt(m2, h_prev.astype(BF16))
                pair.append(y_h)
                last = a_cs_t[h:h + 1, q - 1:q]
                w_row = jnp.exp(last - acs_row) * dt_row
                xw_t = (xs_t[h * SSD_HEADDIM:(h + 1) * SSD_HEADDIM, :] * w_row).astype(BF16)
                st_ref[0, h] = h_prev * jnp.exp(last) + _dot(xw_t, b_g16)
            h0 = g * heads_per_group + r
            yscr[:, h0 * SSD_HEADDIM:(h0 + 2) * SSD_HEADDIM] = jnp.concatenate(pair, axis=1)
    y = yscr[...] + xs * d_ref[...]
    hgate = y * _silu(z_ref[...])
    ms = jnp.mean(hgate * hgate, axis=-1, keepdims=True)
    y_ref[...] = (hgate * lax.rsqrt(ms + LN_EPS) * nw_ref[...]).astype(y_ref.dtype)


def _ssd_scan_prompt(xbc_c, proj, dt_raw, dt_bias, a_log, d_row, norm_w, nb, seq):
    q = SSD_CHUNK
    nc = seq // q
    return pl.pallas_call(
        _ssd_scan_kernel,
        grid=(nb, nc),
        in_specs=[
            pl.BlockSpec((q, SSD_XBC), lambda bi, c: (bi * nc + c, 0)),
            pl.BlockSpec((q, SSD_INNER), lambda bi, c: (bi * nc + c, 0)),
            pl.BlockSpec((q, LANES), lambda bi, c: (bi * nc + c, 0)),
            pl.BlockSpec((1, LANES), lambda bi, c: (0, 0)),
            pl.BlockSpec((1, LANES), lambda bi, c: (0, 0)),
            pl.BlockSpec((1, SSD_INNER), lambda bi, c: (0, 0)),
            pl.BlockSpec((1, SSD_INNER), lambda bi, c: (0, 0)),
        ],
        out_specs=[
            pl.BlockSpec((q, SSD_INNER), lambda bi, c: (bi * nc + c, 0)),
            pl.BlockSpec((1, SSD_HEADS, SSD_HEADDIM, SSD_STATE), lambda bi, c: (bi, 0, 0, 0)),
        ],
        out_shape=[jax.ShapeDtypeStruct((nb * seq, SSD_INNER), BF16),
                   jax.ShapeDtypeStruct((nb, SSD_HEADS, SSD_HEADDIM, SSD_STATE), F32)],
        scratch_shapes=[pltpu.VMEM((q, SSD_INNER), F32)],
        compiler_params=_params("parallel", "arbitrary"),
        name="ssd_scan_prompt",
    )(xbc_c, proj, dt_raw, dt_bias, a_log, d_row, norm_w)


def _ssd_conv_step_kernel(x_ref, st_ref, w_ref, b_ref, o_ref):
    acc = b_ref[...] + w_ref[SSD_CONV - 1:SSD_CONV, :] * x_ref[...]
    for k in range(SSD_CONV - 1):
        acc = acc + w_ref[k:k + 1, :] * st_ref[k]
    o_ref[...] = _silu(acc)


def _ssd_conv_step(proj, state_t, w, b, nb):
    return pl.pallas_call(
        _ssd_conv_step_kernel,
        grid=(SSD_XBC // 1024,),
        in_specs=[
            pl.BlockSpec((nb, 1024), lambda j: (0, COL_XBC + j)),
            pl.BlockSpec((SSD_CONV - 1, nb, 1024), lambda j: (0, 0, j)),
            pl.BlockSpec((SSD_CONV, 1024), lambda j: (0, j)),
            pl.BlockSpec((1, 1024), lambda j: (0, j)),
        ],
        out_specs=pl.BlockSpec((nb, 1024), lambda j: (0, j)),
        out_shape=jax.ShapeDtypeStruct((nb, SSD_XBC), F32),
        compiler_params=_params("parallel"),
        name="ssd_conv_step",
    )(proj, state_t, w, b)


def _ssd_step_kernel(xc_ref, bc_ref, dtraw_ref, dtb_ref, alog_ref, st_ref, st_out_ref, yt_ref, xdt_scr, dec_scr):
    nb = xc_ref.shape[0]
    bi = pl.program_id(0)
    rows = SSD_HEADS * SSD_HEADDIM

    @pl.when(bi == 0)
    def _():
        dt = _softplus(dtraw_ref[...] + dtb_ref[...])
        a = -jnp.exp(alog_ref[...])
        dt_t = dt.T
        da_t = (dt * a).T
        expand = lambda t: jnp.concatenate(
            [jnp.broadcast_to(t[h:h + 1, :], (SSD_HEADDIM, nb)) for h in range(SSD_HEADS)], axis=0)
        xs_t = xc_ref[:, 0:SSD_INNER].T
        xdt_scr[...] = xs_t * expand(dt_t)
        dec_scr[...] = jnp.exp(expand(da_t))
        yt_ref[...] = jnp.zeros_like(yt_ref)

    lane_b = lax.broadcasted_iota(jnp.int32, (nb, LANES), 0)
    onehot = jnp.where(lane_b == bi, 1.0, 0.0).astype(BF16)

    def pick(ref):
        hi, mid, lo = _split3_bf16(ref[...])
        return _dot(hi, onehot) + _dot(mid, onehot) + _dot(lo, onehot)

    xdt_b = pick(xdt_scr)
    dec_b = pick(dec_scr)
    out_lane = lax.broadcasted_iota(jnp.int32, (rows // SSD_GROUPS, LANES), 1) == bi
    for g in range(SSD_GROUPS):
        r0 = g * (rows // SSD_GROUPS)
        r1 = r0 + rows // SSD_GROUPS
        b_row = bc_ref[0, :, g * SSD_STATE:(g + 1) * SSD_STATE]
        c_row = bc_ref[0, :, (SSD_GROUPS + g) * SSD_STATE:(SSD_GROUPS + g + 1) * SSD_STATE]
        h_new = st_ref[0, r0:r1, :] * dec_b[r0:r1, :] + xdt_b[r0:r1, :] * b_row
        st_out_ref[0, r0:r1, :] = h_new
        y_col = jnp.sum(h_new * c_row, axis=-1, keepdims=True)
        yt_ref[r0:r1, :] = jnp.where(out_lane, y_col, yt_ref[r0:r1, :])


def _ssd_step(xbc_c, dt_raw, dt_bias, a_log, state, nb):
    rows = SSD_HEADS * SSD_HEADDIM
    assert nb == LANES
    bc_rows = xbc_c[:, SSD_INNER:].reshape(nb, 1, SSD_XBC - SSD_INNER)
    return pl.pallas_call(
        _ssd_step_kernel,
        grid=(nb,),
        in_specs=[
            pl.BlockSpec((nb, SSD_XBC), lambda bi: (0, 0)),
            pl.BlockSpec((1, 1, bc_rows.shape[2]), lambda bi: (bi, 0, 0)),
            pl.BlockSpec((nb, LANES), lambda bi: (0, 0)),
            pl.BlockSpec((1, LANES), lambda bi: (0, 0)),
            pl.BlockSpec((1, LANES), lambda bi: (0, 0)),
            pl.BlockSpec((1, rows, SSD_STATE), lambda bi: (bi, 0, 0)),
        ],
        out_specs=[
            pl.BlockSpec((1, rows, SSD_STATE), lambda bi: (bi, 0, 0)),
            pl.BlockSpec((rows, nb), lambda bi: (0, 0)),
        ],
        out_shape=[jax.ShapeDtypeStruct((nb, rows, SSD_STATE), F32),
                   jax.ShapeDtypeStruct((rows, nb), F32)],
        scratch_shapes=[pltpu.VMEM((rows, nb), F32), pltpu.VMEM((rows, nb), F32)],
        compiler_params=_params("arbitrary"),
        name="ssd_step",
    )(xbc_c, bc_rows, dt_raw, dt_bias, a_log, state)


def _gated_norm_step_kernel(yt_ref, xc_ref, z_ref, d_ref, nw_ref, o_ref):
    y = yt_ref[...].T + xc_ref[:, 0:SSD_INNER] * d_ref[...]
    hgate = y * _silu(z_ref[...])
    ms = jnp.mean(hgate * hgate, axis=-1, keepdims=True)
    o_ref[...] = (hgate * lax.rsqrt(ms + LN_EPS) * nw_ref[...]).astype(o_ref.dtype)


def _gated_norm_step(y_t, xbc_c, proj, d_row, norm_w, nb):
    return pl.pallas_call(
        _gated_norm_step_kernel,
        grid=(1,),
        in_specs=[
            pl.BlockSpec((SSD_INNER, nb), lambda i: (0, 0)),
            pl.BlockSpec((nb, SSD_XBC), lambda i: (0, 0)),
            pl.BlockSpec((nb, SSD_INNER), lambda i: (0, 0)),
            pl.BlockSpec((1, SSD_INNER), lambda i: (0, 0)),
            pl.BlockSpec((1, SSD_INNER), lambda i: (0, 0)),
        ],
        out_specs=pl.BlockSpec((nb, SSD_INNER), lambda i: (0, 0)),
        out_shape=jax.ShapeDtypeStruct((nb, SSD_INNER), BF16),
        compiler_params=_params("arbitrary"),
        name="ssd_gated_norm_step",
    )(y_t, xbc_c, proj, d_row, norm_w)


CONF_ROWS = 256
CONF_PAD = 32
CONF_W_ROWS = 32


def _conf_kernel(a_ref, b_ref, w_ref, cb_ref, g_ref, beta_ref, o_ref, st_ref, gl, cv):
    tl = a_ref.shape[0]
    l = pl.program_id(1)

    @pl.when(l == 0)
    def _():
        gl[0:CONF_PAD, :] = jnp.zeros((CONF_PAD, CONF_DIM), F32)

    gl[CONF_PAD:CONF_PAD + tl, :] = a_ref[...] * _sigmoid(b_ref[...])
    for r0 in range(0, tl, CONV_CHUNK):
        acc = jnp.broadcast_to(cb_ref[...], (CONV_CHUNK, CONF_DIM))
        for k in range(CONF_WIDTH):
            acc = acc + w_ref[k:k + 1, :] * gl[pl.ds(r0 + CONF_PAD - (CONF_WIDTH - 1) + k, CONV_CHUNK), :]
        cv[r0:r0 + CONV_CHUNK, :] = acc
    o_ref[...] = _silu(_layer_norm_rows(cv[...], g_ref[...], beta_ref[...])).astype(o_ref.dtype)

    @pl.when(l == pl.num_programs(1) - 1)
    def _():
        st_ref[0] = gl[CONF_PAD + tl - (CONF_WIDTH - 1):CONF_PAD + tl, :]

    gl[0:CONF_PAD, :] = gl[tl:tl + CONF_PAD, :]


def _conf_prompt(proj, w, cb, ln_g, ln_b, nb, seq):
    tl = CONF_ROWS
    nl = seq // tl
    vec = pl.BlockSpec((1, CONF_DIM), lambda bi, l: (0, 0))
    return pl.pallas_call(
        _conf_kernel,
        grid=(nb, nl),
        in_specs=[
            pl.BlockSpec((tl, 1024), lambda bi, l: (bi * nl + l, COL_CONF_A)),
            pl.BlockSpec((tl, 1024), lambda bi, l: (bi * nl + l, COL_CONF_B)),
            pl.BlockSpec((CONF_W_ROWS, CONF_DIM), lambda bi, l: (0, 0)),
            vec, vec, vec,
        ],
        out_specs=[
            pl.BlockSpec((tl, CONF_DIM), lambda bi, l: (bi * nl + l, 0)),
            pl.BlockSpec((1, CONF_WIDTH - 1, CONF_DIM), lambda bi, l: (bi, 0, 0)),
        ],
        out_shape=[jax.ShapeDtypeStruct((nb * seq, CONF_DIM), BF16),
                   jax.ShapeDtypeStruct((nb, CONF_WIDTH - 1, CONF_DIM), F32)],
        scratch_shapes=[pltpu.VMEM((CONF_PAD + tl, CONF_DIM), F32), pltpu.VMEM((tl, CONF_DIM), F32)],
        compiler_params=_params("parallel", "arbitrary"),
        name="conf_prompt",
    )(proj, proj, w, cb, ln_g, ln_b)


def _conf_step_kernel(a_ref, b_ref, st_ref, w_ref, cb_ref, g_ref, beta_ref, o_ref, glu_ref):
    glu = a_ref[...] * _sigmoid(b_ref[...])
    glu_ref[...] = glu
    acc = cb_ref[...] + w_ref[CONF_WIDTH - 1:CONF_WIDTH, :] * glu
    for k in range(CONF_WIDTH - 1):
        acc = acc + w_ref[k:k + 1, :] * st_ref[k]
    o_ref[...] = _silu(_layer_norm_rows(acc, g_ref[...], beta_ref[...])).astype(o_ref.dtype)


def _conf_step(proj, state_t, w, cb, ln_g, ln_b, nb):
    vec = pl.BlockSpec((1, CONF_DIM), lambda i: (0, 0))
    return pl.pallas_call(
        _conf_step_kernel,
        grid=(1,),
        in_specs=[
            pl.BlockSpec((nb, 1024), lambda i: (0, COL_CONF_A)),
            pl.BlockSpec((nb, 1024), lambda i: (0, COL_CONF_B)),
            pl.BlockSpec((CONF_WIDTH - 1, nb, CONF_DIM), lambda i: (0, 0, 0)),
            pl.BlockSpec((CONF_W_ROWS, CONF_DIM), lambda i: (0, 0)),
            vec, vec, vec,
        ],
        out_specs=[pl.BlockSpec((nb, CONF_DIM), lambda i: (0, 0)),
                   pl.BlockSpec((nb, CONF_DIM), lambda i: (0, 0))],
        out_shape=[jax.ShapeDtypeStruct((nb, CONF_DIM), BF16),
                   jax.ShapeDtypeStruct((nb, CONF_DIM), F32)],
        compiler_params=_params("arbitrary"),
        name="conf_step",
    )(proj, proj, state_t, w, cb, ln_g, ln_b)


ATTN_ROWS = 512
MEM_SCALE = MEM_HEADDIM ** -0.5


def _softmax_lanes(sc):
    mx = jnp.max(sc, axis=-1, keepdims=True)
    e = jnp.exp(sc - mx)
    return e / jnp.sum(e, axis=-1, keepdims=True)


def _attn_kernel(q_ref, k_ref, v_ref, o_ref):
    for h in range(MEM_HEADS):
        sl = slice(h * MEM_HEADDIM, (h + 1) * MEM_HEADDIM)
        sc = _dot_nt(q_ref[:, sl].astype(BF16), k_ref[:, sl].astype(BF16)) * MEM_SCALE
        pr = _softmax_lanes(sc).astype(BF16)
        o_ref[:, sl] = _dot(pr, v_ref[:, sl].astype(BF16)).astype(o_ref.dtype)


def _attn_prompt(proj, mk, mv, nb, seq):
    tq = ATTN_ROWS
    nq = seq // tq
    kv = pl.BlockSpec((MEM_TOKENS, 1024), lambda bi, i: (bi, 0))
    return pl.pallas_call(
        _attn_kernel,
        grid=(nb, nq),
        in_specs=[pl.BlockSpec((tq, 1024), lambda bi, i: (bi * nq + i, COL_QMEM)), kv, kv],
        out_specs=pl.BlockSpec((tq, 1024), lambda bi, i: (bi * nq + i, 0)),
        out_shape=jax.ShapeDtypeStruct((nb * seq, 1024), BF16),
        compiler_params=_params("parallel", "parallel"),
        name="attn_prompt",
    )(proj, mk, mv)


ATTN_STEP_BATCH = 8


def _attn_step_kernel(q_ref, k_ref, v_ref, o_ref):
    head_of_lane = lax.broadcasted_iota(jnp.int32, (SUBLANES, 1024), 1) // MEM_HEADDIM
    own = head_of_lane == lax.broadcasted_iota(jnp.int32, (SUBLANES, 1024), 0)
    for i in range(ATTN_STEP_BATCH):
        q_row = q_ref[i:i + 1, :]
        q_bd = jnp.where(own, q_row, 0.0).astype(BF16)
        sc = _dot_nt(q_bd, k_ref[i].astype(BF16)) * MEM_SCALE
        pr = _softmax_lanes(sc).astype(BF16)
        o_all = _dot(pr, v_ref[i].astype(BF16))
        o_ref[i:i + 1, :] = jnp.sum(jnp.where(own, o_all, 0.0), axis=0, keepdims=True).astype(o_ref.dtype)


def _attn_step(proj, mem_k, mem_v, nb):
    bb = ATTN_STEP_BATCH
    kv = pl.BlockSpec((bb, MEM_TOKENS, 1024), lambda i: (i, 0, 0))
    return pl.pallas_call(
        _attn_step_kernel,
        grid=(nb // bb,),
        in_specs=[pl.BlockSpec((bb, 1024), lambda i: (i, COL_QMEM)), kv, kv],
        out_specs=pl.BlockSpec((bb, 1024), lambda i: (i, 0)),
        out_shape=jax.ShapeDtypeStruct((nb, 1024), F32),
        compiler_params=_params("parallel"),
        name="attn_step",
    )(proj, mem_k, mem_v)


def _merge_kernel(yn_ref, ca_ref, at_ref, ga_ref, gb_ref, gc_ref, x_ref, wa_ref, wb_ref, wc_ref, wo_ref,
                  g_ref, b_ref, o_ref):
    br_a = _dot(yn_ref[...].astype(BF16), wa_ref[...])
    br_b = _dot(ca_ref[...].astype(BF16), wb_ref[...])
    br_c = _dot(at_ref[...].astype(BF16), wc_ref[...])
    merged = _sigmoid(ga_ref[...]) * br_a + _sigmoid(gb_ref[...]) * br_b + _sigmoid(gc_ref[...]) * br_c
    res = ALPHA * x_ref[...] + _dot(merged.astype(BF16), wo_ref[...])
    o_ref[...] = _layer_norm_rows(res, g_ref[...], b_ref[...])


def _merge(ynorm, cact, attn, proj, x, w_a, w_b, w_c, w_o, ln_g, ln_b, tm):
    t = x.shape[0]
    row = lambda w: pl.BlockSpec((tm, w), lambda i: (i, 0))
    gate = lambda j: pl.BlockSpec((tm, D_MODEL), lambda i: (i, COL_GATES + j))
    full = lambda a: pl.BlockSpec(a.shape, lambda i: (0, 0))
    return pl.pallas_call(
        _merge_kernel,
        grid=(t // tm,),
        in_specs=[row(SSD_INNER), row(CONF_DIM), row(1024), gate(0), gate(1), gate(2),
                  row(D_MODEL), full(w_a), full(w_b), full(w_c), full(w_o), full(ln_g), full(ln_b)],
        out_specs=row(D_MODEL),
        out_shape=jax.ShapeDtypeStruct((t, D_MODEL), F32),
        compiler_params=_params("parallel"),
        name="merge_ln1",
    )(ynorm, cact, attn, proj, proj, proj, x, w_a, w_b, w_c, w_o, ln_g, ln_b)


N_SUBKEYS = 2 * PEER_HEADS
STAIRCASE = [(i, j) for i in range(PEER_TOPK) for j in range(PEER_TOPK) if (i + 1) * (j + 1) <= PEER_TOPK]


def _compare_exchange(v, i, j):
    a, b = v[i], v[j]
    v[i] = jnp.maximum(a, b)
    v[j] = jnp.minimum(a, b)


def _bitonic_merge_desc(v):
    n = len(v)
    j = n // 2
    while j >= 1:
        for i in range(n):
            if i ^ j > i:
                _compare_exchange(v, i, i ^ j)
        j //= 2


def _bitonic_sort_desc(v):
    n = len(v)
    k = 2
    while k <= n:
        j = k // 2
        while j >= 1:
            for i in range(n):
                l = i ^ j
                if l > i:
                    if i & k == 0:
                        _compare_exchange(v, i, l)
                    else:
                        _compare_exchange(v, l, i)
            j //= 2
        k *= 2


def _top16_sorted(s_t):
    v = [s_t[r * SUBLANES:(r + 1) * SUBLANES, :] for r in range(PEER_NKEYS // SUBLANES)]
    _bitonic_sort_desc(v)
    for shift in (4, 2, 1):
        other = [pltpu.roll(x, shift, 0) for x in v]
        v = [jnp.maximum(v[i], other[PEER_TOPK - 1 - i]) for i in range(PEER_TOPK)]
        _bitonic_merge_desc(v)
    return v


def _peer_score_kernel(x_ref, wq_ref, keys_ref, npass_ref, e0_ref, rank_ref, e1_ref, qv_scr, top_scr, s1_scr):
    tm = x_ref.shape[0]
    qv_scr[...] = _dot(x_ref[...].astype(BF16), wq_ref[...]).astype(BF16)
    for hk in range(N_SUBKEYS):
        q_hk = qv_scr[:, hk * PEER_DHALF:(hk + 1) * PEER_DHALF]
        s_t = _dot_nt(keys_ref[hk], q_hk)
        if hk % 2 == 0:
            npass_ref[hk // 2] = s_t
        else:
            s1_scr[hk // 2] = s_t
        top = _top16_sorted(s_t)
        for i in range(PEER_TOPK):
            top_scr[hk % 2, i, hk // 2:hk // 2 + 1, :] = top[i][0:1, :]
    a = [top_scr[0, i] for i in range(PEER_TOPK)]
    b = [top_scr[1, i] for i in range(PEER_TOPK)]
    cand = [a[i] + b[j] for i, j in STAIRCASE]
    tau = jnp.full((PEER_HEADS, tm), -jnp.inf, F32)
    for c in cand:
        cnt = jnp.zeros((PEER_HEADS, tm), F32)
        for c2 in cand:
            cnt = cnt + jnp.where(c2 >= c, 1.0, 0.0)
        tau = jnp.maximum(tau, jnp.where(cnt >= float(PEER_TOPK), c, -jnp.inf))
    top_sum = a[0] + b[0]
    z = jnp.zeros((PEER_HEADS, tm), F32)
    for c in cand:
        z = z + jnp.where(c >= tau, jnp.exp(c - top_sum), 0.0)
    inv_z = 1.0 / z
    for h in range(PEER_HEADS):
        s0 = npass_ref[h]
        s1 = s1_scr[h]
        e0_ref[h] = jnp.exp(s0 - a[0][h:h + 1, :])
        e1_ref[h] = (jnp.exp(s1 - b[0][h:h + 1, :]) * inv_z[h:h + 1, :]).astype(e1_ref.dtype)
        npass = jnp.zeros(s0.shape, F32)
        rank = jnp.zeros(s1.shape, F32)
        for r in range(PEER_TOPK):
            b_r = b[r][h:h + 1, :]
            npass = npass + jnp.where(s0 + b_r >= tau[h:h + 1, :], 1.0, 0.0)
            rank = rank + jnp.where(b_r > s1, 1.0, 0.0)
        npass_ref[h] = npass
        rank_ref[h] = rank.astype(rank_ref.dtype)


def _peer_scores(x, w_q, keys, tm):
    t = x.shape[0]
    spec = pl.BlockSpec((PEER_HEADS, PEER_NKEYS, tm), lambda i: (0, 0, i))
    shape = jax.ShapeDtypeStruct((PEER_HEADS, PEER_NKEYS, t), F32)
    shape16 = jax.ShapeDtypeStruct((PEER_HEADS, PEER_NKEYS, t), BF16)
    return pl.pallas_call(
        _peer_score_kernel,
        grid=(t // tm,),
        in_specs=[pl.BlockSpec((tm, D_MODEL), lambda i: (i, 0)),
                  pl.BlockSpec(w_q.shape, lambda i: (0, 0)),
                  pl.BlockSpec(keys.shape, lambda i: (0, 0, 0))],
        out_specs=[spec, spec, spec, spec],
        out_shape=[shape, shape, shape16, shape16],
        scratch_shapes=[pltpu.VMEM((tm, N_SUBKEYS * PEER_DHALF), BF16),
                        pltpu.VMEM((2, PEER_TOPK, PEER_HEADS, tm), F32),
                        pltpu.VMEM((PEER_HEADS, PEER_NKEYS, tm), F32)],
        compiler_params=_params("parallel"),
        name="peer_scores",
    )(x, w_q, keys)


PEER_UNIT = 1024
I_PER_UNIT = PEER_UNIT // PEER_NKEYS
assert I_PER_UNIT == SUBLANES
UNITS = PEER_EXPERTS // PEER_UNIT
UNITS_PER_STEP = 2
MM_ROWS = 256


def _peer_gate_chunk(act_ref, h_ref, npass_ref, e0_ref, rank_ref, e1_ref, c0, ii):
    cols = slice(c0, c0 + LANES)
    rows = slice(ii * PEER_NKEYS, (ii + 1) * PEER_NKEYS)
    tile = (PEER_NKEYS, LANES)
    gate = jnp.zeros(tile, BF16)
    for h in range(PEER_HEADS):
        npass = jnp.broadcast_to(npass_ref[h, ii:ii + 1, cols], tile).astype(BF16)
        e0 = jnp.broadcast_to(e0_ref[h, ii:ii + 1, cols], tile).astype(BF16)
        sel = rank_ref[h, :, cols] < npass
        gate = gate + e0 * jnp.where(sel, e1_ref[h, :, cols], jnp.zeros(tile, BF16))
    h_ref[rows, cols] = gate * _gelu_erf(act_ref[rows, cols]).astype(BF16)


def _peer_phase(tm, gate_refs, u_ref, u_off, act_w, xb, vt_ref, v_off, h_r, acc_t):
    chunks = [(c0, ii) for c0 in range(0, tm, LANES) for ii in range(I_PER_UNIT)]
    pieces = []
    for r0 in range(0, D_MODEL, MM_ROWS):
        rs = slice(r0, r0 + MM_ROWS)

        def mm2(rs=rs):
            acc_t[rs, :] += _dot(vt_ref[rs, v_off:v_off + PEER_UNIT], h_r[...])
        pieces.append(mm2)
    for r0 in range(0, PEER_UNIT, MM_ROWS):
        def mm1(r0=r0):
            act_w[r0:r0 + MM_ROWS, :] = _dot_nt(u_ref[u_off + r0:u_off + r0 + MM_ROWS, :], xb[...])
        pieces.append(mm1)
    per_piece = -(-len(chunks) // len(pieces))
    for k, piece in enumerate(pieces):
        for c0, ii in chunks[k * per_piece:(k + 1) * per_piece]:
            _peer_gate_chunk(*gate_refs, c0, ii)
        piece()


def _peer_expert_kernel(x_ref, u_ref, vt_ref, np_a, e0_a, np_b, e0_b, rank_ref, e1_ref, g_ref, b_ref, o_ref,
                        acc_t, act0, act1, h0, h1, xb):
    tm = x_ref.shape[0]
    te = PEER_UNIT
    step = pl.program_id(1)

    @pl.when(step == 0)
    def _():
        acc_t[...] = jnp.zeros_like(acc_t)
        act1[...] = jnp.zeros_like(act1)
        h0[...] = jnp.zeros_like(h0)
        h1[...] = jnp.zeros_like(h1)
        xb[...] = x_ref[...].astype(BF16)

    _peer_phase(tm, (act1, h1, np_a, e0_a, rank_ref, e1_ref), u_ref, 0, act0, xb, vt_ref, 0, h0, acc_t)
    _peer_phase(tm, (act0, h0, np_b, e0_b, rank_ref, e1_ref), u_ref, te, act1, xb, vt_ref, te, h1, acc_t)

    @pl.when(step == pl.num_programs(1) - 1)
    def _():
        res = ALPHA * x_ref[...] + acc_t[...].T
        o_ref[...] = _layer_norm_rows(res, g_ref[...], b_ref[...])


def _peer_experts(x, u16, vt16, npass, e0, rank, e1, ln_g, ln_b, tm):
    t = x.shape[0]
    te = PEER_UNIT * UNITS_PER_STEP
    n_load = UNITS // UNITS_PER_STEP
    group = lambda f: pl.BlockSpec((PEER_HEADS, SUBLANES, tm), lambda i, s: (0, jnp.clip(f(s), 0, UNITS - 1), i))
    full = pl.BlockSpec((PEER_HEADS, PEER_NKEYS, tm), lambda i, s: (0, 0, i))
    vec = pl.BlockSpec((1, D_MODEL), lambda i, s: (0, 0))
    prev = lambda s: UNITS_PER_STEP * s - 1
    cur = lambda s: UNITS_PER_STEP * s
    return pl.pallas_call(
        _peer_expert_kernel,
        grid=(t // tm, n_load + 1),
        in_specs=[pl.BlockSpec((tm, D_MODEL), lambda i, s: (i, 0)),
                  pl.BlockSpec((te, D_MODEL), lambda i, s: (jnp.minimum(s, n_load - 1), 0)),
                  pl.BlockSpec((D_MODEL, te), lambda i, s: (0, jnp.maximum(s - 1, 0))),
                  group(prev), group(prev), group(cur), group(cur), full, full, vec, vec],
        out_specs=pl.BlockSpec((tm, D_MODEL), lambda i, s: (i, 0)),
        out_shape=jax.ShapeDtypeStruct((t, D_MODEL), F32),
        scratch_shapes=[pltpu.VMEM((D_MODEL, tm), F32),
                        pltpu.VMEM((PEER_UNIT, tm), F32), pltpu.VMEM((PEER_UNIT, tm), F32),
                        pltpu.VMEM((PEER_UNIT, tm), BF16), pltpu.VMEM((PEER_UNIT, tm), BF16),
                        pltpu.VMEM((tm, D_MODEL), BF16)],
        compiler_params=_params("parallel", "arbitrary"),
        name="peer_experts",
    )(x, u16, vt16, npass, e0, npass, e0, rank, e1, ln_g, ln_b)


def _pad_lanes(v):
    return jnp.pad(v, (0, LANES - v.shape[0])).reshape(1, LANES)


def _prepare_weights(w_in, ssd_conv_w, ssd_conv_b, ssd_dt_bias, ssd_a_log, ssd_d, ssd_norm_w, ssd_w_out,
                     conf_conv_w, conf_conv_b, conf_ln_g, conf_ln_b, conf_w_out, mem_w_k, mem_w_v, mem_w_o,
                     w_out, ln1_g, ln1_b, peer_w_q, peer_sub_keys, peer_u, peer_v, ln2_g, ln2_b):
    w = w_in[0]
    row = lambda v: v[0].reshape(1, -1)
    return dict(
        w_main=jnp.concatenate([w[:, :S_XBC], w[:, S_DT:]], axis=1).astype(BF16),
        w_dt=jnp.pad(w[:, S_XBC:S_DT], ((0, 0), (0, LANES - SSD_HEADS))).astype(BF16),
        ssd_conv_w=ssd_conv_w[0], ssd_conv_b=row(ssd_conv_b),
        dt_bias=_pad_lanes(ssd_dt_bias[0]), a_log=_pad_lanes(ssd_a_log[0]),
        d_row=jnp.repeat(ssd_d[0], SSD_HEADDIM).reshape(1, SSD_INNER), norm_w=row(ssd_norm_w),
        ssd_w_out=ssd_w_out[0].astype(BF16),
        conf_w=jnp.pad(conf_conv_w[0], ((0, CONF_W_ROWS - CONF_WIDTH), (0, 0))), conf_b=row(conf_conv_b),
        conf_ln_g=row(conf_ln_g), conf_ln_b=row(conf_ln_b), conf_w_out=conf_w_out[0].astype(BF16),
        mem_w_k=mem_w_k[0].astype(BF16), mem_w_v=mem_w_v[0].astype(BF16), mem_w_o=mem_w_o[0].astype(BF16),
        w_out=w_out[0].astype(BF16), ln1_g=row(ln1_g), ln1_b=row(ln1_b),
        peer_w_q=peer_w_q[0].astype(BF16),
        peer_keys=peer_sub_keys[0].reshape(N_SUBKEYS, PEER_NKEYS, PEER_DHALF).astype(BF16),
        peer_u=peer_u[0].astype(BF16), peer_vt=peer_v[0].T.astype(BF16),
        ln2_g=row(ln2_g), ln2_b=row(ln2_b),
    )


def _tokenwise_tail(p, x, proj, ynorm, cact, attn, tm_merge, tm_score, tm_expert):
    x1 = _merge(ynorm, cact, attn, proj, x, p["ssd_w_out"], p["conf_w_out"], p["mem_w_o"], p["w_out"],
                p["ln1_g"], p["ln1_b"], tm_merge)
    npass, e0, rank, e1 = _peer_scores(x1, p["peer_w_q"], p["peer_keys"], tm_score)
    return _peer_experts(x1, p["peer_u"], p["peer_vt"], npass, e0, rank, e1, p["ln2_g"], p["ln2_b"], tm_expert)


def _prompt_layer(p, x_prompt, mem_prompt):
    nb, seq, _ = x_prompt.shape
    x = x_prompt.reshape(nb * seq, D_MODEL)
    proj = _matmul(x, p["w_main"], 1024, 1024, "in_proj_prompt")
    dt_raw = _matmul(x, p["w_dt"], 1024, LANES, "dt_proj_prompt")
    mem = mem_prompt.reshape(nb * MEM_TOKENS, D_MODEL)
    mk = _matmul(mem, p["mem_w_k"], 1024, 1024, "mem_k_proj")
    mv = _matmul(mem, p["mem_w_v"], 1024, 1024, "mem_v_proj")
    xbc_c = _ssd_conv_prompt(proj, p["ssd_conv_w"], p["ssd_conv_b"], nb, seq)
    ynorm, ssm = _ssd_scan_prompt(xbc_c, proj, dt_raw, p["dt_bias"], p["a_log"], p["d_row"], p["norm_w"], nb, seq)
    cact, conf_state = _conf_prompt(proj, p["conf_w"], p["conf_b"], p["conf_ln_g"], p["conf_ln_b"], nb, seq)
    attn = _attn_prompt(proj, mk, mv, nb, seq)
    y = _tokenwise_tail(p, x, proj, ynorm, cact, attn, 256, 256, 512)
    ssd_buf = proj.reshape(nb, seq, PROJ_COLS)[:, seq - (SSD_CONV - 1):, S_Z:S_XBC]
    kv_shape = (nb, MEM_TOKENS, MEM_HEADS, MEM_HEADDIM)
    return (y.reshape(nb, seq, D_MODEL), ssm, ssd_buf, conf_state, mk.reshape(kv_shape), mv.reshape(kv_shape))


def _sample_layer(p, x_sample, state_ssd, state_ssd_conv, state_conf_conv, cache_mem_k, cache_mem_v):
    nb = x_sample.shape[0]
    x = x_sample.reshape(nb, D_MODEL)
    proj = _matmul(x, p["w_main"], nb, 1024, "in_proj_sample")
    dt_raw = _matmul(x, p["w_dt"], nb, LANES, "dt_proj_sample")
    xbc_c = _ssd_conv_step(proj, jnp.swapaxes(state_ssd_conv, 0, 1), p["ssd_conv_w"], p["ssd_conv_b"], nb)
    rows = SSD_HEADS * SSD_HEADDIM
    ssm, y_t = _ssd_step(xbc_c, dt_raw, p["dt_bias"], p["a_log"], state_ssd.reshape(nb, rows, SSD_STATE), nb)
    ynorm = _gated_norm_step(y_t, xbc_c, proj, p["d_row"], p["norm_w"], nb)
    cact, glu = _conf_step(proj, jnp.swapaxes(state_conf_conv, 0, 1), p["conf_w"], p["conf_b"],
                           p["conf_ln_g"], p["conf_ln_b"], nb)
    attn = _attn_step(proj, cache_mem_k.reshape(nb, MEM_TOKENS, 1024), cache_mem_v.reshape(nb, MEM_TOKENS, 1024), nb)
    y = _tokenwise_tail(p, x, proj, ynorm, cact, attn, nb, nb, nb)
    ssd_buf = jnp.concatenate([state_ssd_conv[:, 1:], proj[:, None, S_Z:S_XBC]], axis=1)
    conf_buf = jnp.concatenate([state_conf_conv[:, 1:], glu[:, None, :]], axis=1)
    return (y.reshape(nb, 1, D_MODEL), ssm.reshape(nb, SSD_HEADS, SSD_HEADDIM, SSD_STATE), ssd_buf, conf_buf)


def kernel(x_prompt, x_sample, state_ssd, state_ssd_conv, state_conf_conv, cache_mem_k, cache_mem_v, mem_prompt, w_in, ssd_conv_w, ssd_conv_b, ssd_dt_bias, ssd_a_log, ssd_d, ssd_norm_w, ssd_w_out, conf_conv_w, conf_conv_b, conf_ln_g, conf_ln_b, conf_w_out, mem_w_k, mem_w_v, mem_w_o, w_out, ln1_g, ln1_b, peer_w_q, peer_sub_keys, peer_u, peer_v, ln2_g, ln2_b):
    assert w_in.shape[0] == DEPTH == 1
    p = _prepare_weights(w_in, ssd_conv_w, ssd_conv_b, ssd_dt_bias, ssd_a_log, ssd_d, ssd_norm_w, ssd_w_out,
                         conf_conv_w, conf_conv_b, conf_ln_g, conf_ln_b, conf_w_out, mem_w_k, mem_w_v, mem_w_o,
                         w_out, ln1_g, ln1_b, peer_w_q, peer_sub_keys, peer_u, peer_v, ln2_g, ln2_b)
    yp, ssm_p, sbuf_p, cbuf_p, mk_p, mv_p = _prompt_layer(p, x_prompt, mem_prompt)
    ys, ssm_s, sbuf_s, cbuf_s = _sample_layer(p, x_sample, state_ssd[0], state_ssd_conv[0], state_conf_conv[0],
                                              cache_mem_k[0], cache_mem_v[0])
    return (yp, ys, ssm_p[None], sbuf_p[None], cbuf_p[None], mk_p[None], mv_p[None],
            ssm_s[None], sbuf_s[None], cbuf_s[None])
```

```python
import functools
import math

import jax
import jax.numpy as jnp
from jax import lax
from jax.experimental import pallas as pl
from jax.experimental.pallas import tpu as pltpu

F32 = jnp.float32
BF16 = jnp.bfloat16

D_MODEL = 1024
SSD_INNER = 2048
SSD_HEADDIM = 64
SSD_HEADS = 32
SSD_GROUPS = 4
SSD_STATE = 128
SSD_CONV = 4
SSD_CHUNK = 128
SSD_XBC = 3072
CONF_DIM = 1024
CONF_WIDTH = 31
MEM_TOKENS = 256
MEM_HEADS = 4
MEM_HEADDIM = 256
PEER_HEADS = 8
PEER_NKEYS = 128
PEER_EXPERTS = PEER_NKEYS * PEER_NKEYS
PEER_DHALF = 128
PEER_TOPK = 16
DEPTH = 1
ALPHA = (2.0 * DEPTH) ** 0.25
LN_EPS = 1e-5
S_Z = SSD_INNER
S_XBC = S_Z + SSD_XBC
S_DT = S_XBC + SSD_HEADS

COL_XBC = 2
COL_CONF_A = 5
COL_CONF_B = 6
COL_QMEM = 7
COL_GATES = 8
PROJ_COLS = 11 * 1024

LANES = 128
SUBLANES = 8
VMEM_LIMIT_BYTES = 56 * 1024 * 1024


def _params(*sem):
    return pltpu.CompilerParams(dimension_semantics=sem, vmem_limit_bytes=VMEM_LIMIT_BYTES)


def _sigmoid(x):
    return jax.nn.sigmoid(x)


def _silu(x):
    return x * _sigmoid(x)


def _gelu_erf(x):
    return 0.5 * x * (1.0 + lax.erf(x * (1.0 / math.sqrt(2.0))))


def _layer_norm_rows(x, g, b):
    mu = jnp.mean(x, axis=-1, keepdims=True)
    xc = x - mu
    var = jnp.mean(xc * xc, axis=-1, keepdims=True)
    return xc * lax.rsqrt(var + LN_EPS) * g + b


def _split3_bf16(x):
    hi = x.astype(BF16)
    r1 = x - hi.astype(F32)
    mid = r1.astype(BF16)
    lo = (r1 - mid.astype(F32)).astype(BF16)
    return hi, mid, lo


def _dot(a, b):
    return jnp.dot(a, b, preferred_element_type=F32)


def _dot_nt(a, b):
    return lax.dot_general(a, b, (((1,), (1,)), ((), ())), preferred_element_type=F32)


def _mm_kernel(x_ref, w_ref, o_ref):
    o_ref[...] = _dot(x_ref[...].astype(BF16), w_ref[...]).astype(o_ref.dtype)


def _matmul(x, w, tm, tn, name):
    m, k = x.shape
    n = w.shape[1]
    tm = min(tm, m)
    assert m % tm == 0 and n % tn == 0
    return pl.pallas_call(
        _mm_kernel,
        grid=(n // tn, m // tm),
        in_specs=[pl.BlockSpec((tm, k), lambda j, i: (i, 0)),
                  pl.BlockSpec((k, tn), lambda j, i: (0, j))],
        out_specs=pl.BlockSpec((tm, tn), lambda j, i: (i, j)),
        out_shape=jax.ShapeDtypeStruct((m, n), F32),
        compiler_params=_params("parallel", "parallel"),
        name=name,
    )(x, w)


SSD_CONV_ROWS = 512
CONV_CHUNK = 32


def _ssd_conv_kernel(x_ref, halo_ref, w_ref, b_ref, o_ref, scr):
    tl = x_ref.shape[0]
    first = pl.program_id(1) == 0
    scr[0:SUBLANES, :] = jnp.where(first, 0.0, halo_ref[...])
    scr[SUBLANES:SUBLANES + tl, :] = x_ref[...]
    for r0 in range(0, tl, CONV_CHUNK):
        acc = jnp.broadcast_to(b_ref[...], (CONV_CHUNK, x_ref.shape[1]))
        for k in range(SSD_CONV):
            acc = acc + w_ref[k:k + 1, :] * scr[pl.ds(r0 + SUBLANES - (SSD_CONV - 1) + k, CONV_CHUNK), :]
        o_ref[r0:r0 + CONV_CHUNK, :] = _silu(acc)


def _ssd_conv_prompt(proj, w, b, nb, seq):
    tl = SSD_CONV_ROWS
    nl = seq // tl
    rows_per_halo = tl // SUBLANES
    return pl.pallas_call(
        _ssd_conv_kernel,
        grid=(nb, nl, SSD_XBC // 1024),
        in_specs=[
            pl.BlockSpec((tl, 1024), lambda bi, l, j: (bi * nl + l, COL_XBC + j)),
            pl.BlockSpec((SUBLANES, 1024),
                         lambda bi, l, j: (jnp.maximum((bi * nl + l) * rows_per_halo - 1, 0), COL_XBC + j)),
            pl.BlockSpec((SSD_CONV, 1024), lambda bi, l, j: (0, j)),
            pl.BlockSpec((1, 1024), lambda bi, l, j: (0, j)),
        ],
        out_specs=pl.BlockSpec((tl, 1024), lambda bi, l, j: (bi * nl + l, j)),
        out_shape=jax.ShapeDtypeStruct((nb * seq, SSD_XBC), F32),
        scratch_shapes=[pltpu.VMEM((SUBLANES + tl, 1024), F32)],
        compiler_params=_params("parallel", "parallel", "parallel"),
        name="ssd_conv_prompt",
    )(proj, proj, w, b)


def _softplus(x):
    return jnp.maximum(x, 0.0) + jnp.log1p(jnp.exp(-jnp.abs(x)))


def _ssd_scan_kernel(xc_ref, z_ref, dtraw_ref, dtb_ref, alog_ref, d_ref, nw_ref,
                     y_ref, st_ref, yscr):
    q = SSD_CHUNK
    c = pl.program_id(1)

    @pl.when(c == 0)
    def _():
        st_ref[...] = jnp.zeros_like(st_ref)

    dt = _softplus(dtraw_ref[...] + dtb_ref[...])
    a = -jnp.exp(alog_ref[...])
    da = dt * a
    row = lax.broadcasted_iota(jnp.int32, (q, q), 0)
    col = lax.broadcasted_iota(jnp.int32, (q, q), 1)
    causal = row >= col
    tri = jnp.where(causal, 1.0, 0.0).astype(BF16)
    hi, mid, lo = _split3_bf16(da)
    a_cs = _dot(tri, hi) + _dot(tri, mid) + _dot(tri, lo)
    a_cs_t = a_cs.T
    dt_t = dt.T
    xs = xc_ref[:, 0:SSD_INNER]
    xs_t = xs.T
    for g in range(SSD_GROUPS):
        b_g = xc_ref[:, SSD_INNER + g * SSD_STATE:SSD_INNER + (g + 1) * SSD_STATE]
        c_g = xc_ref[:, SSD_INNER + (SSD_GROUPS + g) * SSD_STATE:SSD_INNER + (SSD_GROUPS + g + 1) * SSD_STATE]
        b_g16 = b_g.astype(BF16)
        cb = _dot_nt(c_g.astype(BF16), b_g16)
        heads_per_group = SSD_HEADS // SSD_GROUPS
        for r in range(0, heads_per_group, 2):
            pair = []
            for h in (g * heads_per_group + r, g * heads_per_group + r + 1):
                acs_col = a_cs[:, h:h + 1]
                acs_row = a_cs_t[h:h + 1, :]
                dt_row = dt_t[h:h + 1, :]
                seg = jnp.where(causal, acs_col - acs_row, -1e30)
                m1 = (cb * jnp.exp(seg) * dt_row).astype(BF16)
                m2 = (c_g * jnp.exp(acs_col)).astype(BF16)
                x_h = xs[:, h * SSD_HEADDIM:(h + 1) * SSD_HEADDIM].astype(BF16)
                h_prev = st_ref[0, h]
                y_h = _dot(m1, x_h) + _dot_nt(m2, h_prev.astype(BF16))
                pair.append(y_h)
                last = a_cs_t[h:h + 1, q - 1:q]
                w_row = jnp.exp(last - acs_row) * dt_row
                xw_t = (xs_t[h * SSD_HEADDIM:(h + 1) * SSD_HEADDIM, :] * w_row).astype(BF16)
                st_ref[0, h] = h_prev * jnp.exp(last) + _dot(xw_t, b_g16)
            h0 = g * heads_per_group + r
            yscr[:, h0 * SSD_HEADDIM:(h0 + 2) * SSD_HEADDIM] = jnp.concatenate(pair, axis=1)
    y = yscr[...] + xs * d_ref[...]
    hgate = y * _silu(z_ref[...])
    ms = jnp.mean(hgate * hgate, axis=-1, keepdims=True)
    y_ref[...] = (hgate * lax.rsqrt(ms + LN_EPS) * nw_ref[...]).astype(y_ref.dtype)


def _ssd_scan_prompt(xbc_c, proj, dt_raw, dt_bias, a_log, d_row, norm_w, nb, seq):
    q = SSD_CHUNK
    nc = seq // q
    return pl.pallas_call(
        _ssd_scan_kernel,
        grid=(nb, nc),
        in_specs=[
            pl.BlockSpec((q, SSD_XBC), lambda bi, c: (bi * nc + c, 0)),
            pl.BlockSpec((q, SSD_INNER), lambda bi, c: (bi * nc + c, 0)),
            pl.BlockSpec((q, LANES), lambda bi, c: (bi * nc + c, 0)),
            pl.BlockSpec((1, LANES), lambda bi, c: (0, 0)),
            pl.BlockSpec((1, LANES), lambda bi, c: (0, 0)),
            pl.BlockSpec((1, SSD_INNER), lambda bi, c: (0, 0)),
            pl.BlockSpec((1, SSD_INNER), lambda bi, c: (0, 0)),
        ],
        out_specs=[
            pl.BlockSpec((q, SSD_INNER), lambda bi, c: (bi * nc + c, 0)),
            pl.BlockSpec((1, SSD_HEADS, SSD_HEADDIM, SSD_STATE), lambda bi, c: (bi, 0, 0, 0)),
        ],
        out_shape=[jax.ShapeDtypeStruct((nb * seq, SSD_INNER), BF16),
                   jax.ShapeDtypeStruct((nb, SSD_HEADS, SSD_HEADDIM, SSD_STATE), F32)],
        scratch_shapes=[pltpu.VMEM((q, SSD_INNER), F32)],
        compiler_params=_params("parallel", "arbitrary"),
        name="ssd_scan_prompt",
    )(xbc_c, proj, dt_raw, dt_bias, a_log, d_row, norm_w)


def _ssd_conv_step_kernel(x_ref, st_ref, w_ref, b_ref, o_ref):
    acc = b_ref[...] + w_ref[SSD_CONV - 1:SSD_CONV, :] * x_ref[...]
    for k in range(SSD_CONV - 1):
        acc = acc + w_ref[k:k + 1, :] * st_ref[k]
    o_ref[...] = _silu(acc)


def _ssd_conv_step(proj, state_t, w, b, nb):
    return pl.pallas_call(
        _ssd_conv_step_kernel,
        grid=(SSD_XBC // 1024,),
        in_specs=[
            pl.BlockSpec((nb, 1024), lambda j: (0, COL_XBC + j)),
            pl.BlockSpec((SSD_CONV - 1, nb, 1024), lambda j: (0, 0, j)),
            pl.BlockSpec((SSD_CONV, 1024), lambda j: (0, j)),
            pl.BlockSpec((1, 1024), lambda j: (0, j)),
        ],
        out_specs=pl.BlockSpec((nb, 1024), lambda j: (0, j)),
        out_shape=jax.ShapeDtypeStruct((nb, SSD_XBC), F32),
        compiler_params=_params("parallel"),
        name="ssd_conv_step",
    )(proj, state_t, w, b)


def _ssd_step_kernel(xc_ref, bc_ref, dtraw_ref, dtb_ref, alog_ref, st_ref, st_out_ref, yt_ref, xdt_scr, dec_scr):
    nb = xc_ref.shape[0]
    bi = pl.program_id(0)
    rows = SSD_HEADS * SSD_HEADDIM

    @pl.when(bi == 0)
    def _():
        dt = _softplus(dtraw_ref[...] + dtb_ref[...])
        a = -jnp.exp(alog_ref[...])
        dt_t = dt.T
        da_t = (dt * a).T
        expand = lambda t: jnp.concatenate(
            [jnp.broadcast_to(t[h:h + 1, :], (SSD_HEADDIM, nb)) for h in range(SSD_HEADS)], axis=0)
        xs_t = xc_ref[:, 0:SSD_INNER].T
        xdt_scr[...] = xs_t * expand(dt_t)
        dec_scr[...] = jnp.exp(expand(da_t))
        yt_ref[...] = jnp.zeros_like(yt_ref)

    lane_b = lax.broadcasted_iota(jnp.int32, (nb, LANES), 0)
    onehot = jnp.where(lane_b == bi, 1.0, 0.0).astype(BF16)

    def pick(ref):
        hi, mid, lo = _split3_bf16(ref[...])
        return _dot(hi, onehot) + _dot(mid, onehot) + _dot(lo, onehot)

    xdt_b = pick(xdt_scr)
    dec_b = pick(dec_scr)
    out_lane = lax.broadcasted_iota(jnp.int32, (rows // SSD_GROUPS, LANES), 1) == bi
    for g in range(SSD_GROUPS):
        r0 = g * (rows // SSD_GROUPS)
        r1 = r0 + rows // SSD_GROUPS
        b_row = bc_ref[0, :, g * SSD_STATE:(g + 1) * SSD_STATE]
        c_row = bc_ref[0, :, (SSD_GROUPS + g) * SSD_STATE:(SSD_GROUPS + g + 1) * SSD_STATE]
        h_new = st_ref[0, r0:r1, :] * dec_b[r0:r1, :] + xdt_b[r0:r1, :] * b_row
        st_out_ref[0, r0:r1, :] = h_new
        y_col = jnp.sum(h_new * c_row, axis=-1, keepdims=True)
        yt_ref[r0:r1, :] = jnp.where(out_lane, y_col, yt_ref[r0:r1, :])


def _ssd_step(xbc_c, dt_raw, dt_bias, a_log, state, nb):
    rows = SSD_HEADS * SSD_HEADDIM
    assert nb == LANES
    bc_rows = xbc_c[:, SSD_INNER:].reshape(nb, 1, SSD_XBC - SSD_INNER)
    return pl.pallas_call(
        _ssd_step_kernel,
        grid=(nb,),
        in_specs=[
            pl.BlockSpec((nb, SSD_XBC), lambda bi: (0, 0)),
            pl.BlockSpec((1, 1, bc_rows.shape[2]), lambda bi: (bi, 0, 0)),
            pl.BlockSpec((nb, LANES), lambda bi: (0, 0)),
            pl.BlockSpec((1, LANES), lambda bi: (0, 0)),
            pl.BlockSpec((1, LANES), lambda bi: (0, 0)),
            pl.BlockSpec((1, rows, SSD_STATE), lambda bi: (bi, 0, 0)),
        ],
        out_specs=[
            pl.BlockSpec((1, rows, SSD_STATE), lambda bi: (bi, 0, 0)),
            pl.BlockSpec((rows, nb), lambda bi: (0, 0)),
        ],
        out_shape=[jax.ShapeDtypeStruct((nb, rows, SSD_STATE), F32),
                   jax.ShapeDtypeStruct((rows, nb), F32)],
        scratch_shapes=[pltpu.VMEM((rows, nb), F32), pltpu.VMEM((rows, nb), F32)],
        compiler_params=_params("arbitrary"),
        name="ssd_step",
    )(xbc_c, bc_rows, dt_raw, dt_bias, a_log, state)


def _gated_norm_step_kernel(yt_ref, xc_ref, z_ref, d_ref, nw_ref, o_ref):
    y = yt_ref[...].T + xc_ref[:, 0:SSD_INNER] * d_ref[...]
    hgate = y * _silu(z_ref[...])
    ms = jnp.mean(hgate * hgate, axis=-1, keepdims=True)
    o_ref[...] = (hgate * lax.rsqrt(ms + LN_EPS) * nw_ref[...]).astype(o_ref.dtype)


def _gated_norm_step(y_t, xbc_c, proj, d_row, norm_w, nb):
    return pl.pallas_call(
        _gated_norm_step_kernel,
        grid=(1,),
        in_specs=[
            pl.BlockSpec((SSD_INNER, nb), lambda i: (0, 0)),
            pl.BlockSpec((nb, SSD_XBC), lambda i: (0, 0)),
            pl.BlockSpec((nb, SSD_INNER), lambda i: (0, 0)),
            pl.BlockSpec((1, SSD_INNER), lambda i: (0, 0)),
            pl.BlockSpec((1, SSD_INNER), lambda i: (0, 0)),
        ],
        out_specs=pl.BlockSpec((nb, SSD_INNER), lambda i: (0, 0)),
        out_shape=jax.ShapeDtypeStruct((nb, SSD_INNER), BF16),
        compiler_params=_params("arbitrary"),
        name="ssd_gated_norm_step",
    )(y_t, xbc_c, proj, d_row, norm_w)


CONF_ROWS = 256
CONF_PAD = 32
CONF_W_ROWS = 32


def _conf_kernel(a_ref, b_ref, w_ref, cb_ref, g_ref, beta_ref, o_ref, st_ref, gl, cv):
    tl = a_ref.shape[0]
    l = pl.program_id(1)

    @pl.when(l == 0)
    def _():
        gl[0:CONF_PAD, :] = jnp.zeros((CONF_PAD, CONF_DIM), F32)

    gl[CONF_PAD:CONF_PAD + tl, :] = a_ref[...] * _sigmoid(b_ref[...])
    for r0 in range(0, tl, CONV_CHUNK):
        acc = jnp.broadcast_to(cb_ref[...], (CONV_CHUNK, CONF_DIM))
        for k in range(CONF_WIDTH):
            acc = acc + w_ref[k:k + 1, :] * gl[pl.ds(r0 + CONF_PAD - (CONF_WIDTH - 1) + k, CONV_CHUNK), :]
        cv[r0:r0 + CONV_CHUNK, :] = acc
    o_ref[...] = _silu(_layer_norm_rows(cv[...], g_ref[...], beta_ref[...])).astype(o_ref.dtype)

    @pl.when(l == pl.num_programs(1) - 1)
    def _():
        st_ref[0] = gl[CONF_PAD + tl - (CONF_WIDTH - 1):CONF_PAD + tl, :]

    gl[0:CONF_PAD, :] = gl[tl:tl + CONF_PAD, :]


def _conf_prompt(proj, w, cb, ln_g, ln_b, nb, seq):
    tl = CONF_ROWS
    nl = seq // tl
    vec = pl.BlockSpec((1, CONF_DIM), lambda bi, l: (0, 0))
    return pl.pallas_call(
        _conf_kernel,
        grid=(nb, nl),
        in_specs=[
            pl.BlockSpec((tl, 1024), lambda bi, l: (bi * nl + l, COL_CONF_A)),
            pl.BlockSpec((tl, 1024), lambda bi, l: (bi * nl + l, COL_CONF_B)),
            pl.BlockSpec((CONF_W_ROWS, CONF_DIM), lambda bi, l: (0, 0)),
            vec, vec, vec,
        ],
        out_specs=[
            pl.BlockSpec((tl, CONF_DIM), lambda bi, l: (bi * nl + l, 0)),
            pl.BlockSpec((1, CONF_WIDTH - 1, CONF_DIM), lambda bi, l: (bi, 0, 0)),
        ],
        out_shape=[jax.ShapeDtypeStruct((nb * seq, CONF_DIM), BF16),
                   jax.ShapeDtypeStruct((nb, CONF_WIDTH - 1, CONF_DIM), F32)],
        scratch_shapes=[pltpu.VMEM((CONF_PAD + tl, CONF_DIM), F32), pltpu.VMEM((tl, CONF_DIM), F32)],
        compiler_params=_params("parallel", "arbitrary"),
        name="conf_prompt",
    )(proj, proj, w, cb, ln_g, ln_b)


def _conf_step_kernel(a_ref, b_ref, st_ref, w_ref, cb_ref, g_ref, beta_ref, o_ref, glu_ref):
    glu = a_ref[...] * _sigmoid(b_ref[...])
    glu_ref[...] = glu
    acc = cb_ref[...] + w_ref[CONF_WIDTH - 1:CONF_WIDTH, :] * glu
    for k in range(CONF_WIDTH - 1):
        acc = acc + w_ref[k:k + 1, :] * st_ref[k]
    o_ref[...] = _silu(_layer_norm_rows(acc, g_ref[...], beta_ref[...])).astype(o_ref.dtype)


def _conf_step(proj, state_t, w, cb, ln_g, ln_b, nb):
    vec = pl.BlockSpec((1, CONF_DIM), lambda i: (0, 0))
    return pl.pallas_call(
        _conf_step_kernel,
        grid=(1,),
        in_specs=[
            pl.BlockSpec((nb, 1024), lambda i: (0, COL_CONF_A)),
            pl.BlockSpec((nb, 1024), lambda i: (0, COL_CONF_B)),
            pl.BlockSpec((CONF_WIDTH - 1, nb, CONF_DIM), lambda i: (0, 0, 0)),
            pl.BlockSpec((CONF_W_ROWS, CONF_DIM), lambda i: (0, 0)),
            vec, vec, vec,
        ],
        out_specs=[pl.BlockSpec((nb, CONF_DIM), lambda i: (0, 0)),
                   pl.BlockSpec((nb, CONF_DIM), lambda i: (0, 0))],
        out_shape=[jax.ShapeDtypeStruct((nb, CONF_DIM), BF16),
                   jax.ShapeDtypeStruct((nb, CONF_DIM), F32)],
        compiler_params=_params("arbitrary"),
        name="conf_step",
    )(proj, proj, state_t, w, cb, ln_g, ln_b)


ATTN_ROWS = 512
MEM_SCALE = MEM_HEADDIM ** -0.5


def _softmax_lanes(sc):
    mx = jnp.max(sc, axis=-1, keepdims=True)
    e = jnp.exp(sc - mx)
    return e / jnp.sum(e, axis=-1, keepdims=True)


def _attn_kernel(q_ref, k_ref, v_ref, o_ref):
    for h in range(MEM_HEADS):
        sl = slice(h * MEM_HEADDIM, (h + 1) * MEM_HEADDIM)
        sc = _dot_nt(q_ref[:, sl].astype(BF16), k_ref[:, sl].astype(BF16)) * MEM_SCALE
        pr = _softmax_lanes(sc).astype(BF16)
        o_ref[:, sl] = _dot(pr, v_ref[:, sl].astype(BF16)).astype(o_ref.dtype)


def _attn_prompt(proj, mk, mv, nb, seq):
    tq = ATTN_ROWS
    nq = seq // tq
    kv = pl.BlockSpec((MEM_TOKENS, 1024), lambda bi, i: (bi, 0))
    return pl.pallas_call(
        _attn_kernel,
        grid=(nb, nq),
        in_specs=[pl.BlockSpec((tq, 1024), lambda bi, i: (bi * nq + i, COL_QMEM)), kv, kv],
        out_specs=pl.BlockSpec((tq, 1024), lambda bi, i: (bi * nq + i, 0)),
        out_shape=jax.ShapeDtypeStruct((nb * seq, 1024), BF16),
        compiler_params=_params("parallel", "parallel"),
        name="attn_prompt",
    )(proj, mk, mv)


ATTN_STEP_BATCH = 8


def _attn_step_kernel(q_ref, k_ref, v_ref, o_ref):
    head_of_lane = lax.broadcasted_iota(jnp.int32, (SUBLANES, 1024), 1) // MEM_HEADDIM
    own = head_of_lane == lax.broadcasted_iota(jnp.int32, (SUBLANES, 1024), 0)
    for i in range(ATTN_STEP_BATCH):
        q_row = q_ref[i:i + 1, :]
        q_bd = jnp.where(own, q_row, 0.0).astype(BF16)
        sc = _dot_nt(q_bd, k_ref[i].astype(BF16)) * MEM_SCALE
        pr = _softmax_lanes(sc).astype(BF16)
        o_all = _dot(pr, v_ref[i].astype(BF16))
        o_ref[i:i + 1, :] = jnp.sum(jnp.where(own, o_all, 0.0), axis=0, keepdims=True).astype(o_ref.dtype)


def _attn_step(proj, mem_k, mem_v, nb):
    bb = ATTN_STEP_BATCH
    kv = pl.BlockSpec((bb, MEM_TOKENS, 1024), lambda i: (i, 0, 0))
    return pl.pallas_call(
        _attn_step_kernel,
        grid=(nb // bb,),
        in_specs=[pl.BlockSpec((bb, 1024), lambda i: (i, COL_QMEM)), kv, kv],
        out_specs=pl.BlockSpec((bb, 1024), lambda i: (i, 0)),
        out_shape=jax.ShapeDtypeStruct((nb, 1024), F32),
        compiler_params=_params("parallel"),
        name="attn_step",
    )(proj, mem_k, mem_v)


def _merge_kernel(yn_ref, ca_ref, at_ref, ga_ref, gb_ref, gc_ref, x_ref, wa_ref, wb_ref, wc_ref, wo_ref,
                  g_ref, b_ref, o_ref):
    br_a = _dot(yn_ref[...].astype(BF16), wa_ref[...])
    br_b = _dot(ca_ref[...].astype(BF16), wb_ref[...])
    br_c = _dot(at_ref[...].astype(BF16), wc_ref[...])
    merged = _sigmoid(ga_ref[...]) * br_a + _sigmoid(gb_ref[...]) * br_b + _sigmoid(gc_ref[...]) * br_c
    res = ALPHA * x_ref[...] + _dot(merged.astype(BF16), wo_ref[...])
    o_ref[...] = _layer_norm_rows(res, g_ref[...], b_ref[...])


def _merge(ynorm, cact, attn, proj, x, w_a, w_b, w_c, w_o, ln_g, ln_b, tm):
    t = x.shape[0]
    row = lambda w: pl.BlockSpec((tm, w), lambda i: (i, 0))
    gate = lambda j: pl.BlockSpec((tm, D_MODEL), lambda i: (i, COL_GATES + j))
    full = lambda a: pl.BlockSpec(a.shape, lambda i: (0, 0))
    return pl.pallas_call(
        _merge_kernel,
        grid=(t // tm,),
        in_specs=[row(SSD_INNER), row(CONF_DIM), row(1024), gate(0), gate(1), gate(2),
                  row(D_MODEL), full(w_a), full(w_b), full(w_c), full(w_o), full(ln_g), full(ln_b)],
        out_specs=row(D_MODEL),
        out_shape=jax.ShapeDtypeStruct((t, D_MODEL), F32),
        compiler_params=_params("parallel"),
        name="merge_ln1",
    )(ynorm, cact, attn, proj, proj, proj, x, w_a, w_b, w_c, w_o, ln_g, ln_b)


N_SUBKEYS = 2 * PEER_HEADS
STAIRCASE = [(i, j) for i in range(PEER_TOPK) for j in range(PEER_TOPK) if (i + 1) * (j + 1) <= PEER_TOPK]


def _compare_exchange(v, i, j):
    a, b = v[i], v[j]
    v[i] = jnp.maximum(a, b)
    v[j] = jnp.minimum(a, b)


def _bitonic_merge_desc(v):
    n = len(v)
    j = n // 2
    while j >= 1:
        for i in range(n):
            if i ^ j > i:
                _compare_exchange(v, i, i ^ j)
        j //= 2


def _bitonic_sort_desc(v):
    n = len(v)
    k = 2
    while k <= n:
        j = k // 2
        while j >= 1:
            for i in range(n):
                l = i ^ j
                if l > i:
                    if i & k == 0:
                        _compare_exchange(v, i, l)
                    else:
                        _compare_exchange(v, l, i)
            j //= 2
        k *= 2


def _top16_sorted(s_t):
    v = [s_t[r * SUBLANES:(r + 1) * SUBLANES, :] for r in range(PEER_NKEYS // SUBLANES)]
    _bitonic_sort_desc(v)
    for shift in (4, 2, 1):
        other = [pltpu.roll(x, shift, 0) for x in v]
        v = [jnp.maximum(v[i], other[PEER_TOPK - 1 - i]) for i in range(PEER_TOPK)]
        _bitonic_merge_desc(v)
    return v


def _prefix_length(test, vals):
    w = jnp.where
    c8 = test(vals[7])
    c4 = test(w(c8, vals[11], vals[3]))
    c2 = test(w(c8, w(c4, vals[13], vals[9]), w(c4, vals[5], vals[1])))
    c1 = test(w(c8, w(c4, w(c2, vals[14], vals[12]), w(c2, vals[10], vals[8])),
                w(c4, w(c2, vals[6], vals[4]), w(c2, vals[2], vals[0]))))
    c16 = test(vals[15])
    return w(c8, 8.0, 0.0) + w(c4, 4.0, 0.0) + w(c2, 2.0, 0.0) + w(c1, 1.0, 0.0) + w(c16, 1.0, 0.0)


def _pack_factor():
    return 4 // jnp.dtype(BF16).itemsize


def _pack_rows(x):
    return pltpu.bitcast(x.astype(BF16), jnp.uint32)


def _unpack_rows(w):
    return pltpu.bitcast(w, BF16)


def _peer_score_kernel(x_ref, wq_ref, keys_ref, npass_ref, e0_ref, rank_ref, e1_ref, qv_scr, top_scr, s1_scr):
    tm = x_ref.shape[0]
    qv_scr[...] = _dot(x_ref[...].astype(BF16), wq_ref[...]).astype(BF16)
    for hk in range(N_SUBKEYS):
        q_hk = qv_scr[:, hk * PEER_DHALF:(hk + 1) * PEER_DHALF]
        s_t = _dot_nt(keys_ref[hk], q_hk)
        if hk % 2 == 0:
            npass_ref[hk // 2] = s_t
        else:
            s1_scr[hk // 2] = s_t
        top = _top16_sorted(s_t)
        for i in range(PEER_TOPK):
            top_scr[hk % 2, i, hk // 2:hk // 2 + 1, :] = top[i][0:1, :]
    a = [top_scr[0, i] for i in range(PEER_TOPK)]
    b = [top_scr[1, i] for i in range(PEER_TOPK)]
    cand = [a[i] + b[j] for i, j in STAIRCASE]
    ordered = cand + [jnp.full((PEER_HEADS, tm), -jnp.inf, F32)] * (64 - len(cand))
    _bitonic_sort_desc(ordered)
    tau = ordered[PEER_TOPK - 1]
    top_sum = a[0] + b[0]
    z = jnp.zeros((PEER_HEADS, tm), F32)
    for c in cand:
        z = z + jnp.where(c >= tau, jnp.exp(c - top_sum), 0.0)
    inv_z = 1.0 / z
    for h in range(PEER_HEADS):
        s0 = npass_ref[h]
        s1 = s1_scr[h]
        e0_ref[h] = jnp.exp(s0 - a[0][h:h + 1, :])
        e1_ref[h] = _pack_rows(jnp.exp(s1 - b[0][h:h + 1, :]) * inv_z[h:h + 1, :])
        b_rows = [b[r][h:h + 1, :] for r in range(PEER_TOPK)]
        tau_h = tau[h:h + 1, :]
        npass_ref[h] = _prefix_length(lambda v: s0 + v >= tau_h, b_rows)
        rank_ref[h] = _pack_rows(_prefix_length(lambda v: v > s1, b_rows))


def _peer_scores(x, w_q, keys, tm):
    t = x.shape[0]
    spec = pl.BlockSpec((PEER_HEADS, PEER_NKEYS, tm), lambda i: (0, 0, i))
    spec16 = pl.BlockSpec((PEER_HEADS, PEER_NKEYS // _pack_factor(), tm), lambda i: (0, 0, i))
    shape = jax.ShapeDtypeStruct((PEER_HEADS, PEER_NKEYS, t), F32)
    shape16 = jax.ShapeDtypeStruct((PEER_HEADS, PEER_NKEYS // _pack_factor(), t), jnp.uint32)
    return pl.pallas_call(
        _peer_score_kernel,
        grid=(t // tm,),
        in_specs=[pl.BlockSpec((tm, D_MODEL), lambda i: (i, 0)),
                  pl.BlockSpec(w_q.shape, lambda i: (0, 0)),
                  pl.BlockSpec(keys.shape, lambda i: (0, 0, 0))],
        out_specs=[spec, spec, spec16, spec16],
        out_shape=[shape, shape, shape16, shape16],
        scratch_shapes=[pltpu.VMEM((tm, N_SUBKEYS * PEER_DHALF), BF16),
                        pltpu.VMEM((2, PEER_TOPK, PEER_HEADS, tm), F32),
                        pltpu.VMEM((PEER_HEADS, PEER_NKEYS, tm), F32)],
        compiler_params=_params("parallel"),
        name="peer_scores",
    )(x, w_q, keys)


PEER_UNIT = 1024
I_PER_UNIT = PEER_UNIT // PEER_NKEYS
assert I_PER_UNIT == SUBLANES
UNITS = PEER_EXPERTS // PEER_UNIT
UNITS_PER_STEP = 2
II_GROUP = 2


def _peer_gate_chunk(act_ref, h_ref, npass_ref, e0_ref, rank_ref, e1_ref, c0, ii0):
    gate_sub = SUBLANES * _pack_factor()
    cols = slice(c0, c0 + LANES)
    sub = (gate_sub, LANES)
    zero = jnp.zeros(sub, BF16)
    row = lambda ref, h, ii: jnp.broadcast_to(ref[h, ii:ii + 1, cols], sub).astype(BF16)
    npass = [[row(npass_ref, h, ii0 + q) for h in range(PEER_HEADS)] for q in range(II_GROUP)]
    e0 = [[row(e0_ref, h, ii0 + q) for h in range(PEER_HEADS)] for q in range(II_GROUP)]
    for j0 in range(0, PEER_NKEYS, gate_sub):
        words = slice(j0 // _pack_factor(), j0 // _pack_factor() + SUBLANES)
        gate = [zero] * II_GROUP
        for h in range(PEER_HEADS):
            rank = _unpack_rows(rank_ref[h, words, cols])
            e1 = _unpack_rows(e1_ref[h, words, cols])
            for q in range(II_GROUP):
                gate[q] = gate[q] + e0[q][h] * jnp.where(rank < npass[q][h], e1, zero)
        for q in range(II_GROUP):
            r0 = (ii0 + q) * PEER_NKEYS + j0
            h_ref[r0:r0 + gate_sub, cols] = gate[q] * _gelu_erf(act_ref[r0:r0 + gate_sub, cols]).astype(BF16)


def _peer_phase(tm, gate_refs, u_ref, u_off, act_w, xb, vt_ref, v_off, h_r, acc_t):
    for c0 in range(0, tm, LANES):
        for ii in range(0, I_PER_UNIT, II_GROUP):
            _peer_gate_chunk(*gate_refs, c0, ii)
    acc_t[...] += _dot(vt_ref[:, v_off:v_off + PEER_UNIT], h_r[...])
    act_w[...] = _dot_nt(u_ref[u_off:u_off + PEER_UNIT, :], xb[...])


def _peer_expert_kernel(x_ref, u_ref, vt_ref, np_a, e0_a, np_b, e0_b, rank_ref, e1_ref, g_ref, b_ref, o_ref,
                        acc_t, act0, act1, h0, h1, xb):
    tm = x_ref.shape[0]
    te = PEER_UNIT
    step = pl.program_id(1)

    @pl.when(step == 0)
    def _():
        acc_t[...] = jnp.zeros_like(acc_t)
        act1[...] = jnp.zeros_like(act1)
        h0[...] = jnp.zeros_like(h0)
        h1[...] = jnp.zeros_like(h1)
        xb[...] = x_ref[...].astype(BF16)

    _peer_phase(tm, (act1, h1, np_a, e0_a, rank_ref, e1_ref), u_ref, 0, act0, xb, vt_ref, 0, h0, acc_t)
    _peer_phase(tm, (act0, h0, np_b, e0_b, rank_ref, e1_ref), u_ref, te, act1, xb, vt_ref, te, h1, acc_t)

    @pl.when(step == pl.num_programs(1) - 1)
    def _():
        res = ALPHA * x_ref[...] + acc_t[...].T
        o_ref[...] = _layer_norm_rows(res, g_ref[...], b_ref[...])


def _peer_experts(x, u16, vt16, npass, e0, rank, e1, ln_g, ln_b, tm):
    t = x.shape[0]
    te = PEER_UNIT * UNITS_PER_STEP
    n_load = UNITS // UNITS_PER_STEP
    group = lambda f: pl.BlockSpec((PEER_HEADS, SUBLANES, tm), lambda i, s: (0, jnp.clip(f(s), 0, UNITS - 1), i))
    full = pl.BlockSpec((PEER_HEADS, PEER_NKEYS // _pack_factor(), tm), lambda i, s: (0, 0, i))
    vec = pl.BlockSpec((1, D_MODEL), lambda i, s: (0, 0))
    prev = lambda s: UNITS_PER_STEP * s - 1
    cur = lambda s: UNITS_PER_STEP * s
    return pl.pallas_call(
        _peer_expert_kernel,
        grid=(t // tm, n_load + 1),
        in_specs=[pl.BlockSpec((tm, D_MODEL), lambda i, s: (i, 0)),
                  pl.BlockSpec((te, D_MODEL), lambda i, s: (jnp.minimum(s, n_load - 1), 0)),
                  pl.BlockSpec((D_MODEL, te), lambda i, s: (0, jnp.maximum(s - 1, 0))),
                  group(prev), group(prev), group(cur), group(cur), full, full, vec, vec],
        out_specs=pl.BlockSpec((tm, D_MODEL), lambda i, s: (i, 0)),
        out_shape=jax.ShapeDtypeStruct((t, D_MODEL), F32),
        scratch_shapes=[pltpu.VMEM((D_MODEL, tm), F32),
                        pltpu.VMEM((PEER_UNIT, tm), F32), pltpu.VMEM((PEER_UNIT, tm), F32),
                        pltpu.VMEM((PEER_UNIT, tm), BF16), pltpu.VMEM((PEER_UNIT, tm), BF16),
                        pltpu.VMEM((tm, D_MODEL), BF16)],
        compiler_params=_params("parallel", "arbitrary"),
        name="peer_experts",
    )(x, u16, vt16, npass, e0, npass, e0, rank, e1, ln_g, ln_b)


def _pad_lanes(v):
    return jnp.pad(v, (0, LANES - v.shape[0])).reshape(1, LANES)


def _prepare_weights(w_in, ssd_conv_w, ssd_conv_b, ssd_dt_bias, ssd_a_log, ssd_d, ssd_norm_w, ssd_w_out,
                     conf_conv_w, conf_conv_b, conf_ln_g, conf_ln_b, conf_w_out, mem_w_k, mem_w_v, mem_w_o,
                     w_out, ln1_g, ln1_b, peer_w_q, peer_sub_keys, peer_u, peer_v, ln2_g, ln2_b):
    w = w_in[0]
    row = lambda v: v[0].reshape(1, -1)
    return dict(
        w_main=jnp.concatenate([w[:, :S_XBC], w[:, S_DT:]], axis=1).astype(BF16),
        w_dt=jnp.pad(w[:, S_XBC:S_DT], ((0, 0), (0, LANES - SSD_HEADS))).astype(BF16),
        ssd_conv_w=ssd_conv_w[0], ssd_conv_b=row(ssd_conv_b),
        dt_bias=_pad_lanes(ssd_dt_bias[0]), a_log=_pad_lanes(ssd_a_log[0]),
        d_row=jnp.repeat(ssd_d[0], SSD_HEADDIM).reshape(1, SSD_INNER), norm_w=row(ssd_norm_w),
        ssd_w_out=ssd_w_out[0].astype(BF16),
        conf_w=jnp.pad(conf_conv_w[0], ((0, CONF_W_ROWS - CONF_WIDTH), (0, 0))), conf_b=row(conf_conv_b),
        conf_ln_g=row(conf_ln_g), conf_ln_b=row(conf_ln_b), conf_w_out=conf_w_out[0].astype(BF16),
        mem_w_k=mem_w_k[0].astype(BF16), mem_w_v=mem_w_v[0].astype(BF16), mem_w_o=mem_w_o[0].astype(BF16),
        w_out=w_out[0].astype(BF16), ln1_g=row(ln1_g), ln1_b=row(ln1_b),
        peer_w_q=peer_w_q[0].astype(BF16),
        peer_keys=peer_sub_keys[0].reshape(N_SUBKEYS, PEER_NKEYS, PEER_DHALF).astype(BF16),
        peer_u=peer_u[0].astype(BF16), peer_vt=peer_v[0].T.astype(BF16),
        ln2_g=row(ln2_g), ln2_b=row(ln2_b),
    )


def _tokenwise_tail(p, x, proj, ynorm, cact, attn, tm_merge, tm_score, tm_expert):
    x1 = _merge(ynorm, cact, attn, proj, x, p["ssd_w_out"], p["conf_w_out"], p["mem_w_o"], p["w_out"],
                p["ln1_g"], p["ln1_b"], tm_merge)
    npass, e0, rank, e1 = _peer_scores(x1, p["peer_w_q"], p["peer_keys"], tm_score)
    return _peer_experts(x1, p["peer_u"], p["peer_vt"], npass, e0, rank, e1, p["ln2_g"], p["ln2_b"], tm_expert)


def _prompt_layer(p, x_prompt, mem_prompt):
    nb, seq, _ = x_prompt.shape
    x = x_prompt.reshape(nb * seq, D_MODEL)
    proj = _matmul(x, p["w_main"], 1024, 1024, "in_proj_prompt")
    dt_raw = _matmul(x, p["w_dt"], 1024, LANES, "dt_proj_prompt")
    mem = mem_prompt.reshape(nb * MEM_TOKENS, D_MODEL)
    mk = _matmul(mem, p["mem_w_k"], 1024, 1024, "mem_k_proj")
    mv = _matmul(mem, p["mem_w_v"], 1024, 1024, "mem_v_proj")
    xbc_c = _ssd_conv_prompt(proj, p["ssd_conv_w"], p["ssd_conv_b"], nb, seq)
    ynorm, ssm = _ssd_scan_prompt(xbc_c, proj, dt_raw, p["dt_bias"], p["a_log"], p["d_row"], p["norm_w"], nb, seq)
    cact, conf_state = _conf_prompt(proj, p["conf_w"], p["conf_b"], p["conf_ln_g"], p["conf_ln_b"], nb, seq)
    attn = _attn_prompt(proj, mk, mv, nb, seq)
    y = _tokenwise_tail(p, x, proj, ynorm, cact, attn, 256, 256, 512)
    ssd_buf = proj.reshape(nb, seq, PROJ_COLS)[:, seq - (SSD_CONV - 1):, S_Z:S_XBC]
    kv_shape = (nb, MEM_TOKENS, MEM_HEADS, MEM_HEADDIM)
    return (y.reshape(nb, seq, D_MODEL), ssm, ssd_buf, conf_state, mk.reshape(kv_shape), mv.reshape(kv_shape))


def _sample_layer(p, x_sample, state_ssd, state_ssd_conv, state_conf_conv, cache_mem_k, cache_mem_v):
    nb = x_sample.shape[0]
    x = x_sample.reshape(nb, D_MODEL)
    proj = _matmul(x, p["w_main"], nb, 1024, "in_proj_sample")
    dt_raw = _matmul(x, p["w_dt"], nb, LANES, "dt_proj_sample")
    xbc_c = _ssd_conv_step(proj, jnp.swapaxes(state_ssd_conv, 0, 1), p["ssd_conv_w"], p["ssd_conv_b"], nb)
    rows = SSD_HEADS * SSD_HEADDIM
    ssm, y_t = _ssd_step(xbc_c, dt_raw, p["dt_bias"], p["a_log"], state_ssd.reshape(nb, rows, SSD_STATE), nb)
    ynorm = _gated_norm_step(y_t, xbc_c, proj, p["d_row"], p["norm_w"], nb)
    cact, glu = _conf_step(proj, jnp.swapaxes(state_conf_conv, 0, 1), p["conf_w"], p["conf_b"],
                           p["conf_ln_g"], p["conf_ln_b"], nb)
    attn = _attn_step(proj, cache_mem_k.reshape(nb, MEM_TOKENS, 1024), cache_mem_v.reshape(nb, MEM_TOKENS, 1024), nb)
    y = _tokenwise_tail(p, x, proj, ynorm, cact, attn, nb, nb, nb)
    ssd_buf = jnp.concatenate([state_ssd_conv[:, 1:], proj[:, None, S_Z:S_XBC]], axis=1)
    conf_buf = jnp.concatenate([state_conf_conv[:, 1:], glu[:, None, :]], axis=1)
    return (y.reshape(nb, 1, D_MODEL), ssm.reshape(nb, SSD_HEADS, SSD_HEADDIM, SSD_STATE), ssd_buf, conf_buf)


def kernel(x_prompt, x_sample, state_ssd, state_ssd_conv, state_conf_conv, cache_mem_k, cache_mem_v, mem_prompt, w_in, ssd_conv_w, ssd_conv_b, ssd_dt_bias, ssd_a_log, ssd_d, ssd_norm_w, ssd_w_out, conf_conv_w, conf_conv_b, conf_ln_g, conf_ln_b, conf_w_out, mem_w_k, mem_w_v, mem_w_o, w_out, ln1_g, ln1_b, peer_w_q, peer_sub_keys, peer_u, peer_v, ln2_g, ln2_b):
    assert w_in.shape[0] == DEPTH == 1
    p = _prepare_weights(w_in, ssd_conv_w, ssd_conv_b, ssd_dt_bias, ssd_a_log, ssd_d, ssd_norm_w, ssd_w_out,
                         conf_conv_w, conf_conv_b, conf_ln_g, conf_ln_b, conf_w_out, mem_w_k, mem_w_v, mem_w_o,
                         w_out, ln1_g, ln1_b, peer_w_q, peer_sub_keys, peer_u, peer_v, ln2_g, ln2_b)
    yp, ssm_p, sbuf_p, cbuf_p, mk_p, mv_p = _prompt_layer(p, x_prompt, mem_prompt)
    ys, ssm_s, sbuf_s, cbuf_s = _sample_layer(p, x_sample, state_ssd[0], state_ssd_conv[0], state_conf_conv[0],
                                              cache_mem_k[0], cache_mem_v[0])
    return (yp, ys, ssm_p[None], sbuf_p[None], cbuf_p[None], mk_p[None], mv_p[None],
            ssm_s[None], sbuf_s[None], cbuf_s[None])
```

```python
import functools
import math

import jax
import jax.numpy as jnp
from jax import lax
from jax.experimental import pallas as pl
from jax.experimental.pallas import tpu as pltpu

F32 = jnp.float32
BF16 = jnp.bfloat16

D_MODEL = 1024
SSD_INNER = 2048
SSD_HEADDIM = 64
SSD_HEADS = 32
SSD_GROUPS = 4
SSD_STATE = 128
SSD_CONV = 4
SSD_CHUNK = 128
SSD_XBC = 3072
CONF_DIM = 1024
CONF_WIDTH = 31
MEM_TOKENS = 256
MEM_HEADS = 4
MEM_HEADDIM = 256
PEER_HEADS = 8
PEER_NKEYS = 128
PEER_EXPERTS = PEER_NKEYS * PEER_NKEYS
PEER_DHALF = 128
PEER_TOPK = 16
DEPTH = 1
ALPHA = (2.0 * DEPTH) ** 0.25
LN_EPS = 1e-5
S_Z = SSD_INNER
S_XBC = S_Z + SSD_XBC
S_DT = S_XBC + SSD_HEADS

COL_XBC = 2
COL_CONF_A = 5
COL_CONF_B = 6
COL_QMEM = 7
COL_GATES = 8
PROJ_COLS = 11 * 1024

LANES = 128
SUBLANES = 8
VMEM_LIMIT_BYTES = 56 * 1024 * 1024


def _params(*sem):
    return pltpu.CompilerParams(dimension_semantics=sem, vmem_limit_bytes=VMEM_LIMIT_BYTES)


def _sigmoid(x):
    return jax.nn.sigmoid(x)


def _silu(x):
    return x * _sigmoid(x)


def _twice_gelu_erf(x):
    return x * (1.0 + lax.erf(x * (1.0 / math.sqrt(2.0))))


def _layer_norm_rows(x, g, b):
    mu = jnp.mean(x, axis=-1, keepdims=True)
    xc = x - mu
    var = jnp.mean(xc * xc, axis=-1, keepdims=True)
    return xc * lax.rsqrt(var + LN_EPS) * g + b


def _split3_bf16(x):
    hi = x.astype(BF16)
    r1 = x - hi.astype(F32)
    mid = r1.astype(BF16)
    lo = (r1 - mid.astype(F32)).astype(BF16)
    return hi, mid, lo


def _dot(a, b):
    return jnp.dot(a, b, preferred_element_type=F32)


def _dot_nt(a, b):
    return lax.dot_general(a, b, (((1,), (1,)), ((), ())), preferred_element_type=F32)


def _mm_kernel(x_ref, w_ref, o_ref):
    o_ref[...] = _dot(x_ref[...].astype(BF16), w_ref[...]).astype(o_ref.dtype)


def _matmul(x, w, tm, tn, name):
    m, k = x.shape
    n = w.shape[1]
    tm = min(tm, m)
    assert m % tm == 0 and n % tn == 0
    return pl.pallas_call(
        _mm_kernel,
        grid=(m // tm, n // tn),
        in_specs=[pl.BlockSpec((tm, k), lambda i, j: (i, 0)),
                  pl.BlockSpec((k, tn), lambda i, j: (0, j))],
        out_specs=pl.BlockSpec((tm, tn), lambda i, j: (i, j)),
        out_shape=jax.ShapeDtypeStruct((m, n), F32),
        compiler_params=_params("parallel", "parallel"),
        name=name,
    )(x, w)


SSD_CONV_ROWS = 512
CONV_CHUNK = 32


def _causal_conv_rows(src, r0, cols, w_ref, bias_ref, taps, first_off):
    win = -(-(first_off + taps - 1 + CONV_CHUNK) // SUBLANES) * SUBLANES
    window = src[r0:r0 + win, cols]
    acc = jnp.broadcast_to(bias_ref[:, cols], (CONV_CHUNK, LANES))
    for rho in range(SUBLANES):
        ks = [k for k in range(taps) if (first_off + k) % SUBLANES == rho]
        if not ks:
            continue
        shifted = window if rho == 0 else pltpu.roll(window, win - rho, 0)
        for k in ks:
            a = (first_off + k) // SUBLANES * SUBLANES
            acc = acc + w_ref[k:k + 1, cols] * shifted[a:a + CONV_CHUNK, :]
    return acc


def _ssd_conv_kernel(x_ref, halo_ref, w_ref, b_ref, o_ref, scr):
    tl = x_ref.shape[0]
    first = pl.program_id(1) == 0
    scr[0:SUBLANES, :] = jnp.where(first, 0.0, halo_ref[...])
    scr[SUBLANES:SUBLANES + tl, :] = x_ref[...]
    for r0 in range(0, tl, CONV_CHUNK):
        for c0 in range(0, x_ref.shape[1], LANES):
            cols = slice(c0, c0 + LANES)
            acc = _causal_conv_rows(scr, r0, cols, w_ref, b_ref, SSD_CONV, SUBLANES - (SSD_CONV - 1))
            o_ref[r0:r0 + CONV_CHUNK, cols] = _silu(acc)


def _ssd_conv_prompt(proj, w, b, nb, seq):
    tl = SSD_CONV_ROWS
    nl = seq // tl
    rows_per_halo = tl // SUBLANES
    return pl.pallas_call(
        _ssd_conv_kernel,
        grid=(nb, nl, SSD_XBC // 1024),
        in_specs=[
            pl.BlockSpec((tl, 1024), lambda bi, l, j: (bi * nl + l, COL_XBC + j)),
            pl.BlockSpec((SUBLANES, 1024),
                         lambda bi, l, j: (jnp.maximum((bi * nl + l) * rows_per_halo - 1, 0), COL_XBC + j)),
            pl.BlockSpec((SSD_CONV, 1024), lambda bi, l, j: (0, j)),
            pl.BlockSpec((1, 1024), lambda bi, l, j: (0, j)),
        ],
        out_specs=pl.BlockSpec((tl, 1024), lambda bi, l, j: (bi * nl + l, j)),
        out_shape=jax.ShapeDtypeStruct((nb * seq, SSD_XBC), F32),
        scratch_shapes=[pltpu.VMEM((SUBLANES + tl, 1024), F32)],
        compiler_params=_params("parallel", "parallel", "parallel"),
        name="ssd_conv_prompt",
    )(proj, proj, w, b)


def _softplus(x):
    return jnp.maximum(x, 0.0) + jnp.log1p(jnp.exp(-jnp.abs(x)))


def _ssd_scan_kernel(xc_ref, z_ref, dtraw_ref, dtb_ref, alog_ref, d_ref, nw_ref,
                     y_ref, st_ref, yscr):
    q = SSD_CHUNK
    c = pl.program_id(1)

    @pl.when(c == 0)
    def _():
        st_ref[...] = jnp.zeros_like(st_ref)

    dt = _softplus(dtraw_ref[...] + dtb_ref[...])
    a = -jnp.exp(alog_ref[...])
    da = dt * a
    row = lax.broadcasted_iota(jnp.int32, (q, q), 0)
    col = lax.broadcasted_iota(jnp.int32, (q, q), 1)
    causal = row >= col
    tri = jnp.where(causal, 1.0, 0.0).astype(BF16)
    hi, mid, lo = _split3_bf16(da)
    a_cs = _dot(tri, hi) + _dot(tri, mid) + _dot(tri, lo)
    a_cs_t = a_cs.T
    dt_t = dt.T
    xs = xc_ref[:, 0:SSD_INNER]
    xs_t = xs.T
    for g in range(SSD_GROUPS):
        b_g = xc_ref[:, SSD_INNER + g * SSD_STATE:SSD_INNER + (g + 1) * SSD_STATE]
        c_g = xc_ref[:, SSD_INNER + (SSD_GROUPS + g) * SSD_STATE:SSD_INNER + (SSD_GROUPS + g + 1) * SSD_STATE]
        b_g16 = b_g.astype(BF16)
        cb = _dot_nt(c_g.astype(BF16), b_g16)
        heads_per_group = SSD_HEADS // SSD_GROUPS
        for r in range(0, heads_per_group, 2):
            pair = []
            for h in (g * heads_per_group + r, g * heads_per_group + r + 1):
                acs_col = a_cs[:, h:h + 1]
                acs_row = a_cs_t[h:h + 1, :]
                dt_row = dt_t[h:h + 1, :]
                seg = jnp.where(causal, acs_col - acs_row, -1e30)
                m1 = (cb * jnp.exp(seg) * dt_row).astype(BF16)
                m2 = (c_g * jnp.exp(acs_col)).astype(BF16)
                x_h = xs[:, h * SSD_HEADDIM:(h + 1) * SSD_HEADDIM].astype(BF16)
                h_prev = st_ref[0, h]
                y_h = _dot(m1, x_h) + _dot_nt(m2, h_prev.astype(BF16))
                pair.append(y_h)
                last = a_cs_t[h:h + 1, q - 1:q]
                w_row = jnp.exp(last - acs_row) * dt_row
                xw_t = (xs_t[h * SSD_HEADDIM:(h + 1) * SSD_HEADDIM, :] * w_row).astype(BF16)
                st_ref[0, h] = h_prev * jnp.exp(last) + _dot(xw_t, b_g16)
            h0 = g * heads_per_group + r
            yscr[:, h0 * SSD_HEADDIM:(h0 + 2) * SSD_HEADDIM] = jnp.concatenate(pair, axis=1)
    y = yscr[...] + xs * d_ref[...]
    hgate = y * _silu(z_ref[...])
    ms = jnp.mean(hgate * hgate, axis=-1, keepdims=True)
    y_ref[...] = (hgate * lax.rsqrt(ms + LN_EPS) * nw_ref[...]).astype(y_ref.dtype)


def _ssd_scan_prompt(xbc_c, proj, dt_raw, dt_bias, a_log, d_row, norm_w, nb, seq):
    q = SSD_CHUNK
    nc = seq // q
    return pl.pallas_call(
        _ssd_scan_kernel,
        grid=(nb, nc),
        in_specs=[
            pl.BlockSpec((q, SSD_XBC), lambda bi, c: (bi * nc + c, 0)),
            pl.BlockSpec((q, SSD_INNER), lambda bi, c: (bi * nc + c, 0)),
            pl.BlockSpec((q, LANES), lambda bi, c: (bi * nc + c, 0)),
            pl.BlockSpec((1, LANES), lambda bi, c: (0, 0)),
            pl.BlockSpec((1, LANES), lambda bi, c: (0, 0)),
            pl.BlockSpec((1, SSD_INNER), lambda bi, c: (0, 0)),
            pl.BlockSpec((1, SSD_INNER), lambda bi, c: (0, 0)),
        ],
        out_specs=[
            pl.BlockSpec((q, SSD_INNER), lambda bi, c: (bi * nc + c, 0)),
            pl.BlockSpec((1, SSD_HEADS, SSD_HEADDIM, SSD_STATE), lambda bi, c: (bi, 0, 0, 0)),
        ],
        out_shape=[jax.ShapeDtypeStruct((nb * seq, SSD_INNER), BF16),
                   jax.ShapeDtypeStruct((nb, SSD_HEADS, SSD_HEADDIM, SSD_STATE), F32)],
        scratch_shapes=[pltpu.VMEM((q, SSD_INNER), F32)],
        compiler_params=_params("parallel", "arbitrary"),
        name="ssd_scan_prompt",
    )(xbc_c, proj, dt_raw, dt_bias, a_log, d_row, norm_w)


def _ssd_conv_step_kernel(x_ref, st_ref, w_ref, b_ref, o_ref):
    acc = b_ref[...] + w_ref[SSD_CONV - 1:SSD_CONV, :] * x_ref[...]
    for k in range(SSD_CONV - 1):
        acc = acc + w_ref[k:k + 1, :] * st_ref[k]
    o_ref[...] = _silu(acc)


def _ssd_conv_step(proj, state_t, w, b, nb):
    return pl.pallas_call(
        _ssd_conv_step_kernel,
        grid=(SSD_XBC // 1024,),
        in_specs=[
            pl.BlockSpec((nb, 1024), lambda j: (0, COL_XBC + j)),
            pl.BlockSpec((SSD_CONV - 1, nb, 1024), lambda j: (0, 0, j)),
            pl.BlockSpec((SSD_CONV, 1024), lambda j: (0, j)),
            pl.BlockSpec((1, 1024), lambda j: (0, j)),
        ],
        out_specs=pl.BlockSpec((nb, 1024), lambda j: (0, j)),
        out_shape=jax.ShapeDtypeStruct((nb, SSD_XBC), F32),
        compiler_params=_params("parallel"),
        name="ssd_conv_step",
    )(proj, state_t, w, b)


def _ssd_step_kernel(xc_ref, bc_ref, dtraw_ref, dtb_ref, alog_ref, st_ref, st_out_ref, yt_ref, xdt_scr, dec_scr):
    nb = xc_ref.shape[0]
    bi = pl.program_id(0)
    rows = SSD_HEADS * SSD_HEADDIM

    @pl.when(bi == 0)
    def _():
        dt = _softplus(dtraw_ref[...] + dtb_ref[...])
        a = -jnp.exp(alog_ref[...])
        dt_t = dt.T
        da_t = (dt * a).T
        expand = lambda t: jnp.concatenate(
            [jnp.broadcast_to(t[h:h + 1, :], (SSD_HEADDIM, nb)) for h in range(SSD_HEADS)], axis=0)
        xs_t = xc_ref[:, 0:SSD_INNER].T
        xdt_scr[...] = xs_t * expand(dt_t)
        dec_scr[...] = jnp.exp(expand(da_t))
        yt_ref[...] = jnp.zeros_like(yt_ref)

    lane_b = lax.broadcasted_iota(jnp.int32, (nb, LANES), 0)
    onehot = jnp.where(lane_b == bi, 1.0, 0.0).astype(BF16)

    def pick(ref):
        hi, mid, lo = _split3_bf16(ref[...])
        return _dot(hi, onehot) + _dot(mid, onehot) + _dot(lo, onehot)

    xdt_b = pick(xdt_scr)
    dec_b = pick(dec_scr)
    out_lane = lax.broadcasted_iota(jnp.int32, (rows // SSD_GROUPS, LANES), 1) == bi
    for g in range(SSD_GROUPS):
        r0 = g * (rows // SSD_GROUPS)
        r1 = r0 + rows // SSD_GROUPS
        b_row = bc_ref[0, :, g * SSD_STATE:(g + 1) * SSD_STATE]
        c_row = bc_ref[0, :, (SSD_GROUPS + g) * SSD_STATE:(SSD_GROUPS + g + 1) * SSD_STATE]
        h_new = st_ref[0, r0:r1, :] * dec_b[r0:r1, :] + xdt_b[r0:r1, :] * b_row
        st_out_ref[0, r0:r1, :] = h_new
        y_col = jnp.sum(h_new * c_row, axis=-1, keepdims=True)
        yt_ref[r0:r1, :] = jnp.where(out_lane, y_col, yt_ref[r0:r1, :])


def _ssd_step(xbc_c, dt_raw, dt_bias, a_log, state, nb):
    rows = SSD_HEADS * SSD_HEADDIM
    assert nb == LANES
    bc_rows = xbc_c[:, SSD_INNER:].reshape(nb, 1, SSD_XBC - SSD_INNER)
    return pl.pallas_call(
        _ssd_step_kernel,
        grid=(nb,),
        in_specs=[
            pl.BlockSpec((nb, SSD_XBC), lambda bi: (0, 0)),
            pl.BlockSpec((1, 1, bc_rows.shape[2]), lambda bi: (bi, 0, 0)),
            pl.BlockSpec((nb, LANES), lambda bi: (0, 0)),
            pl.BlockSpec((1, LANES), lambda bi: (0, 0)),
            pl.BlockSpec((1, LANES), lambda bi: (0, 0)),
            pl.BlockSpec((1, rows, SSD_STATE), lambda bi: (bi, 0, 0)),
        ],
        out_specs=[
            pl.BlockSpec((1, rows, SSD_STATE), lambda bi: (bi, 0, 0)),
            pl.BlockSpec((rows, nb), lambda bi: (0, 0)),
        ],
        out_shape=[jax.ShapeDtypeStruct((nb, rows, SSD_STATE), F32),
                   jax.ShapeDtypeStruct((rows, nb), F32)],
        scratch_shapes=[pltpu.VMEM((rows, nb), F32), pltpu.VMEM((rows, nb), F32)],
        compiler_params=_params("arbitrary"),
        name="ssd_step",
    )(xbc_c, bc_rows, dt_raw, dt_bias, a_log, state)


def _gated_norm_step_kernel(yt_ref, xc_ref, z_ref, d_ref, nw_ref, o_ref):
    y = yt_ref[...].T + xc_ref[:, 0:SSD_INNER] * d_ref[...]
    hgate = y * _silu(z_ref[...])
    ms = jnp.mean(hgate * hgate, axis=-1, keepdims=True)
    o_ref[...] = (hgate * lax.rsqrt(ms + LN_EPS) * nw_ref[...]).astype(o_ref.dtype)


def _gated_norm_step(y_t, xbc_c, proj, d_row, norm_w, nb):
    return pl.pallas_call(
        _gated_norm_step_kernel,
        grid=(1,),
        in_specs=[
            pl.BlockSpec((SSD_INNER, nb), lambda i: (0, 0)),
            pl.BlockSpec((nb, SSD_XBC), lambda i: (0, 0)),
            pl.BlockSpec((nb, SSD_INNER), lambda i: (0, 0)),
            pl.BlockSpec((1, SSD_INNER), lambda i: (0, 0)),
            pl.BlockSpec((1, SSD_INNER), lambda i: (0, 0)),
        ],
        out_specs=pl.BlockSpec((nb, SSD_INNER), lambda i: (0, 0)),
        out_shape=jax.ShapeDtypeStruct((nb, SSD_INNER), BF16),
        compiler_params=_params("arbitrary"),
        name="ssd_gated_norm_step",
    )(y_t, xbc_c, proj, d_row, norm_w)


CONF_ROWS = 256
CONF_PAD = 32
CONF_W_ROWS = 32


def _conf_kernel(a_ref, b_ref, w_ref, cb_ref, g_ref, beta_ref, o_ref, st_ref, gl, cv):
    tl = a_ref.shape[0]
    l = pl.program_id(1)

    @pl.when(l == 0)
    def _():
        gl[0:CONF_PAD, :] = jnp.zeros((CONF_PAD, CONF_DIM), F32)

    gl[CONF_PAD:CONF_PAD + tl, :] = a_ref[...] * _sigmoid(b_ref[...])
    for r0 in range(0, tl, CONV_CHUNK):
        for c0 in range(0, CONF_DIM, LANES):
            cols = slice(c0, c0 + LANES)
            cv[r0:r0 + CONV_CHUNK, cols] = _causal_conv_rows(gl, r0, cols, w_ref, cb_ref, CONF_WIDTH,
                                                             CONF_PAD - (CONF_WIDTH - 1))
    o_ref[...] = _silu(_layer_norm_rows(cv[...], g_ref[...], beta_ref[...])).astype(o_ref.dtype)

    @pl.when(l == pl.num_programs(1) - 1)
    def _():
        st_ref[0] = gl[CONF_PAD + tl - (CONF_WIDTH - 1):CONF_PAD + tl, :]

    gl[0:CONF_PAD, :] = gl[tl:tl + CONF_PAD, :]


def _conf_prompt(proj, w, cb, ln_g, ln_b, nb, seq):
    tl = CONF_ROWS
    nl = seq // tl
    vec = pl.BlockSpec((1, CONF_DIM), lambda bi, l: (0, 0))
    return pl.pallas_call(
        _conf_kernel,
        grid=(nb, nl),
        in_specs=[
            pl.BlockSpec((tl, 1024), lambda bi, l: (bi * nl + l, COL_CONF_A)),
            pl.BlockSpec((tl, 1024), lambda bi, l: (bi * nl + l, COL_CONF_B)),
            pl.BlockSpec((CONF_W_ROWS, CONF_DIM), lambda bi, l: (0, 0)),
            vec, vec, vec,
        ],
        out_specs=[
            pl.BlockSpec((tl, CONF_DIM), lambda bi, l: (bi * nl + l, 0)),
            pl.BlockSpec((1, CONF_WIDTH - 1, CONF_DIM), lambda bi, l: (bi, 0, 0)),
        ],
        out_shape=[jax.ShapeDtypeStruct((nb * seq, CONF_DIM), BF16),
                   jax.ShapeDtypeStruct((nb, CONF_WIDTH - 1, CONF_DIM), F32)],
        scratch_shapes=[pltpu.VMEM((CONF_PAD + tl, CONF_DIM), F32), pltpu.VMEM((tl, CONF_DIM), F32)],
        compiler_params=_params("parallel", "arbitrary"),
        name="conf_prompt",
    )(proj, proj, w, cb, ln_g, ln_b)


def _conf_step_kernel(a_ref, b_ref, st_ref, w_ref, cb_ref, g_ref, beta_ref, o_ref, glu_ref):
    glu = a_ref[...] * _sigmoid(b_ref[...])
    glu_ref[...] = glu
    acc = cb_ref[...] + w_ref[CONF_WIDTH - 1:CONF_WIDTH, :] * glu
    for k in range(CONF_WIDTH - 1):
        acc = acc + w_ref[k:k + 1, :] * st_ref[k]
    o_ref[...] = _silu(_layer_norm_rows(acc, g_ref[...], beta_ref[...])).astype(o_ref.dtype)


def _conf_step(proj, state_t, w, cb, ln_g, ln_b, nb):
    vec = pl.BlockSpec((1, CONF_DIM), lambda i: (0, 0))
    return pl.pallas_call(
        _conf_step_kernel,
        grid=(1,),
        in_specs=[
            pl.BlockSpec((nb, 1024), lambda i: (0, COL_CONF_A)),
            pl.BlockSpec((nb, 1024), lambda i: (0, COL_CONF_B)),
            pl.BlockSpec((CONF_WIDTH - 1, nb, CONF_DIM), lambda i: (0, 0, 0)),
            pl.BlockSpec((CONF_W_ROWS, CONF_DIM), lambda i: (0, 0)),
            vec, vec, vec,
        ],
        out_specs=[pl.BlockSpec((nb, CONF_DIM), lambda i: (0, 0)),
                   pl.BlockSpec((nb, CONF_DIM), lambda i: (0, 0))],
        out_shape=[jax.ShapeDtypeStruct((nb, CONF_DIM), BF16),
                   jax.ShapeDtypeStruct((nb, CONF_DIM), F32)],
        compiler_params=_params("arbitrary"),
        name="conf_step",
    )(proj, proj, state_t, w, cb, ln_g, ln_b)


ATTN_ROWS = 512
MEM_SCALE = MEM_HEADDIM ** -0.5


def _softmax_lanes(sc):
    mx = jnp.max(sc, axis=-1, keepdims=True)
    e = jnp.exp(sc - mx)
    return e / jnp.sum(e, axis=-1, keepdims=True)


def _attn_kernel(q_ref, k_ref, v_ref, o_ref):
    for h in range(MEM_HEADS):
        sl = slice(h * MEM_HEADDIM, (h + 1) * MEM_HEADDIM)
        sc = _dot_nt(q_ref[:, sl].astype(BF16), k_ref[:, sl].astype(BF16)) * MEM_SCALE
        pr = _softmax_lanes(sc).astype(BF16)
        o_ref[:, sl] = _dot(pr, v_ref[:, sl].astype(BF16)).astype(o_ref.dtype)


def _attn_prompt(proj, mk, mv, nb, seq):
    tq = ATTN_ROWS
    nq = seq // tq
    kv = pl.BlockSpec((MEM_TOKENS, 1024), lambda bi, i: (bi, 0))
    return pl.pallas_call(
        _attn_kernel,
        grid=(nb, nq),
        in_specs=[pl.BlockSpec((tq, 1024), lambda bi, i: (bi * nq + i, COL_QMEM)), kv, kv],
        out_specs=pl.BlockSpec((tq, 1024), lambda bi, i: (bi * nq + i, 0)),
        out_shape=jax.ShapeDtypeStruct((nb * seq, 1024), BF16),
        compiler_params=_params("parallel", "parallel"),
        name="attn_prompt",
    )(proj, mk, mv)


ATTN_STEP_BATCH = 8


def _attn_step_kernel(q_ref, k_ref, v_ref, o_ref):
    head_of_lane = lax.broadcasted_iota(jnp.int32, (SUBLANES, 1024), 1) // MEM_HEADDIM
    own = head_of_lane == lax.broadcasted_iota(jnp.int32, (SUBLANES, 1024), 0)
    for i in range(ATTN_STEP_BATCH):
        q_row = q_ref[i:i + 1, :]
        q_bd = jnp.where(own, q_row, 0.0).astype(BF16)
        sc = _dot_nt(q_bd, k_ref[i].astype(BF16)) * MEM_SCALE
        pr = _softmax_lanes(sc).astype(BF16)
        o_all = _dot(pr, v_ref[i].astype(BF16))
        o_ref[i:i + 1, :] = jnp.sum(jnp.where(own, o_all, 0.0), axis=0, keepdims=True).astype(o_ref.dtype)


def _attn_step(proj, mem_k, mem_v, nb):
    bb = ATTN_STEP_BATCH
    kv = pl.BlockSpec((bb, MEM_TOKENS, 1024), lambda i: (i, 0, 0))
    return pl.pallas_call(
        _attn_step_kernel,
        grid=(nb // bb,),
        in_specs=[pl.BlockSpec((bb, 1024), lambda i: (i, COL_QMEM)), kv, kv],
        out_specs=pl.BlockSpec((bb, 1024), lambda i: (i, 0)),
        out_shape=jax.ShapeDtypeStruct((nb, 1024), F32),
        compiler_params=_params("parallel"),
        name="attn_step",
    )(proj, mem_k, mem_v)


def _merge_kernel(yn_ref, ca_ref, at_ref, ga_ref, gb_ref, gc_ref, x_ref, wa_ref, wb_ref, wc_ref, wo_ref,
                  g_ref, b_ref, o_ref):
    br_a = _dot(yn_ref[...].astype(BF16), wa_ref[...])
    br_b = _dot(ca_ref[...].astype(BF16), wb_ref[...])
    br_c = _dot(at_ref[...].astype(BF16), wc_ref[...])
    merged = _sigmoid(ga_ref[...]) * br_a + _sigmoid(gb_ref[...]) * br_b + _sigmoid(gc_ref[...]) * br_c
    res = ALPHA * x_ref[...] + _dot(merged.astype(BF16), wo_ref[...])
    o_ref[...] = _layer_norm_rows(res, g_ref[...], b_ref[...])


def _merge(ynorm, cact, attn, proj, x, w_a, w_b, w_c, w_o, ln_g, ln_b, tm):
    t = x.shape[0]
    row = lambda w: pl.BlockSpec((tm, w), lambda i: (i, 0))
    gate = lambda j: pl.BlockSpec((tm, D_MODEL), lambda i: (i, COL_GATES + j))
    full = lambda a: pl.BlockSpec(a.shape, lambda i: (0, 0))
    return pl.pallas_call(
        _merge_kernel,
        grid=(t // tm,),
        in_specs=[row(SSD_INNER), row(CONF_DIM), row(1024), gate(0), gate(1), gate(2),
                  row(D_MODEL), full(w_a), full(w_b), full(w_c), full(w_o), full(ln_g), full(ln_b)],
        out_specs=row(D_MODEL),
        out_shape=jax.ShapeDtypeStruct((t, D_MODEL), F32),
        compiler_params=_params("parallel"),
        name="merge_ln1",
    )(ynorm, cact, attn, proj, proj, proj, x, w_a, w_b, w_c, w_o, ln_g, ln_b)


N_SUBKEYS = 2 * PEER_HEADS
STAIRCASE = [(i, j) for i in range(PEER_TOPK) for j in range(PEER_TOPK) if (i + 1) * (j + 1) <= PEER_TOPK]


def _compare_exchange(v, i, j):
    a, b = v[i], v[j]
    v[i] = jnp.maximum(a, b)
    v[j] = jnp.minimum(a, b)


def _bitonic_merge_desc(v):
    n = len(v)
    j = n // 2
    while j >= 1:
        for i in range(n):
            if i ^ j > i:
                _compare_exchange(v, i, i ^ j)
        j //= 2


def _bitonic_sort_desc(v):
    n = len(v)
    k = 2
    while k <= n:
        j = k // 2
        while j >= 1:
            for i in range(n):
                l = i ^ j
                if l > i:
                    if i & k == 0:
                        _compare_exchange(v, i, l)
                    else:
                        _compare_exchange(v, l, i)
            j //= 2
        k *= 2


def _top16_sorted(s_t):
    v = [s_t[r * SUBLANES:(r + 1) * SUBLANES, :] for r in range(PEER_NKEYS // SUBLANES)]
    _bitonic_sort_desc(v)
    for shift in (4, 2, 1):
        other = [pltpu.roll(x, shift, 0) for x in v]
        v = [jnp.maximum(v[i], other[PEER_TOPK - 1 - i]) for i in range(PEER_TOPK)]
        _bitonic_merge_desc(v)
    return v


def _prefix_length(test, vals):
    w = jnp.where
    c8 = test(vals[7])
    c4 = test(w(c8, vals[11], vals[3]))
    c2 = test(w(c8, w(c4, vals[13], vals[9]), w(c4, vals[5], vals[1])))
    c1 = test(w(c8, w(c4, w(c2, vals[14], vals[12]), w(c2, vals[10], vals[8])),
                w(c4, w(c2, vals[6], vals[4]), w(c2, vals[2], vals[0]))))
    c16 = test(vals[15])
    return w(c8, 8.0, 0.0) + w(c4, 4.0, 0.0) + w(c2, 2.0, 0.0) + w(c1, 1.0, 0.0) + w(c16, 1.0, 0.0)


def _pack_factor():
    return 4 // jnp.dtype(BF16).itemsize


def _pack_rows(x):
    return pltpu.bitcast(x.astype(BF16), jnp.uint32)


def _unpack_rows(w):
    return pltpu.bitcast(w, BF16)


def _peer_score_kernel(x_ref, wq_ref, keys_ref, npass_ref, e0_ref, rank_ref, e1_ref, qv_scr, top_scr, s1_scr):
    tm = x_ref.shape[0]
    qv_scr[...] = _dot(x_ref[...].astype(BF16), wq_ref[...]).astype(BF16)
    for hk in range(N_SUBKEYS):
        q_hk = qv_scr[:, hk * PEER_DHALF:(hk + 1) * PEER_DHALF]
        s_t = _dot_nt(keys_ref[hk], q_hk)
        if hk % 2 == 0:
            npass_ref[hk // 2] = s_t
        else:
            s1_scr[hk // 2] = s_t
        top = _top16_sorted(s_t)
        for i in range(PEER_TOPK):
            top_scr[hk % 2, i, hk // 2:hk // 2 + 1, :] = top[i][0:1, :]
    a = [top_scr[0, i] for i in range(PEER_TOPK)]
    b = [top_scr[1, i] for i in range(PEER_TOPK)]
    cand = [a[i] + b[j] for i, j in STAIRCASE]
    ordered = cand + [jnp.full((PEER_HEADS, tm), -jnp.inf, F32)] * (64 - len(cand))
    _bitonic_sort_desc(ordered)
    tau = ordered[PEER_TOPK - 1]
    top_sum = a[0] + b[0]
    z = jnp.zeros((PEER_HEADS, tm), F32)
    for c in cand:
        z = z + jnp.where(c >= tau, jnp.exp(c - top_sum), 0.0)
    half_inv_z = 0.5 / z
    for h in range(PEER_HEADS):
        s0 = npass_ref[h]
        s1 = s1_scr[h]
        e0_ref[h] = jnp.exp(s0 - a[0][h:h + 1, :])
        e1_ref[h] = _pack_rows(jnp.exp(s1 - b[0][h:h + 1, :]) * half_inv_z[h:h + 1, :])
        b_rows = [b[r][h:h + 1, :] for r in range(PEER_TOPK)]
        tau_h = tau[h:h + 1, :]
        npass_ref[h] = _prefix_length(lambda v: s0 + v >= tau_h, b_rows)
        rank_ref[h] = _pack_rows(_prefix_length(lambda v: v > s1, b_rows))


def _peer_scores(x, w_q, keys, tm):
    t = x.shape[0]
    spec = pl.BlockSpec((PEER_HEADS, PEER_NKEYS, tm), lambda i: (0, 0, i))
    spec16 = pl.BlockSpec((PEER_HEADS, PEER_NKEYS // _pack_factor(), tm), lambda i: (0, 0, i))
    shape = jax.ShapeDtypeStruct((PEER_HEADS, PEER_NKEYS, t), F32)
    shape16 = jax.ShapeDtypeStruct((PEER_HEADS, PEER_NKEYS // _pack_factor(), t), jnp.uint32)
    return pl.pallas_call(
        _peer_score_kernel,
        grid=(t // tm,),
        in_specs=[pl.BlockSpec((tm, D_MODEL), lambda i: (i, 0)),
                  pl.BlockSpec(w_q.shape, lambda i: (0, 0)),
                  pl.BlockSpec(keys.shape, lambda i: (0, 0, 0))],
        out_specs=[spec, spec, spec16, spec16],
        out_shape=[shape, shape, shape16, shape16],
        scratch_shapes=[pltpu.VMEM((tm, N_SUBKEYS * PEER_DHALF), BF16),
                        pltpu.VMEM((2, PEER_TOPK, PEER_HEADS, tm), F32),
                        pltpu.VMEM((PEER_HEADS, PEER_NKEYS, tm), F32)],
        compiler_params=_params("parallel"),
        name="peer_scores",
    )(x, w_q, keys)


PEER_UNIT = 1024
I_PER_UNIT = PEER_UNIT // PEER_NKEYS
assert I_PER_UNIT == SUBLANES
UNITS = PEER_EXPERTS // PEER_UNIT
UNITS_PER_STEP = 2
II_GROUP = 2


def _peer_gate_chunk(act_ref, h_ref, npass_ref, e0_ref, rank_ref, e1_ref, c0, ii0):
    gate_sub = SUBLANES * _pack_factor()
    cols = slice(c0, c0 + LANES)
    sub = (gate_sub, LANES)
    zero = jnp.zeros(sub, BF16)
    row = lambda ref, h, ii: jnp.broadcast_to(ref[h, ii:ii + 1, cols], sub).astype(BF16)
    npass = [[row(npass_ref, h, ii0 + q) for h in range(PEER_HEADS)] for q in range(II_GROUP)]
    e0 = [[row(e0_ref, h, ii0 + q) for h in range(PEER_HEADS)] for q in range(II_GROUP)]
    for j0 in range(0, PEER_NKEYS, gate_sub):
        words = slice(j0 // _pack_factor(), j0 // _pack_factor() + SUBLANES)
        gate = [zero] * II_GROUP
        for h in range(PEER_HEADS):
            rank = _unpack_rows(rank_ref[h, words, cols])
            e1 = _unpack_rows(e1_ref[h, words, cols])
            for q in range(II_GROUP):
                gate[q] = gate[q] + e0[q][h] * jnp.where(rank < npass[q][h], e1, zero)
        for q in range(II_GROUP):
            r0 = (ii0 + q) * PEER_NKEYS + j0
            h_ref[r0:r0 + gate_sub, cols] = gate[q] * _twice_gelu_erf(act_ref[r0:r0 + gate_sub, cols]).astype(BF16)


def _peer_unit(tm, u_ref, vt_ref, off, npass_ref, e0_ref, rank_ref, e1_ref, xb, act, h, acc_t):
    act[...] = _dot_nt(u_ref[off:off + PEER_UNIT, :], xb[...])
    for c0 in range(0, tm, LANES):
        for ii in range(0, I_PER_UNIT, II_GROUP):
            _peer_gate_chunk(act, h, npass_ref, e0_ref, rank_ref, e1_ref, c0, ii)
    acc_t[...] += _dot(vt_ref[:, off:off + PEER_UNIT], h[...])


def _peer_expert_kernel(x_ref, u_ref, vt_ref, np_a, e0_a, np_b, e0_b, rank_ref, e1_ref, g_ref, b_ref, o_ref,
                        acc_t, act0, act1, h0, h1, xb):
    tm = x_ref.shape[0]
    step = pl.program_id(1)

    @pl.when(step == 0)
    def _():
        acc_t[...] = jnp.zeros_like(acc_t)
        xb[...] = x_ref[...].astype(BF16)

    _peer_unit(tm, u_ref, vt_ref, 0, np_a, e0_a, rank_ref, e1_ref, xb, act0, h0, acc_t)
    _peer_unit(tm, u_ref, vt_ref, PEER_UNIT, np_b, e0_b, rank_ref, e1_ref, xb, act1, h1, acc_t)

    @pl.when(step == pl.num_programs(1) - 1)
    def _():
        res = ALPHA * x_ref[...] + acc_t[...].T
        o_ref[...] = _layer_norm_rows(res, g_ref[...], b_ref[...])


def _peer_experts(x, u16, vt16, npass, e0, rank, e1, ln_g, ln_b, tm):
    t = x.shape[0]
    te = PEER_UNIT * UNITS_PER_STEP
    group = lambda q: pl.BlockSpec((PEER_HEADS, SUBLANES, tm), lambda i, s: (0, UNITS_PER_STEP * s + q, i))
    full = pl.BlockSpec((PEER_HEADS, PEER_NKEYS // _pack_factor(), tm), lambda i, s: (0, 0, i))
    vec = pl.BlockSpec((1, D_MODEL), lambda i, s: (0, 0))
    return pl.pallas_call(
        _peer_expert_kernel,
        grid=(t // tm, UNITS // UNITS_PER_STEP),
        in_specs=[pl.BlockSpec((tm, D_MODEL), lambda i, s: (i, 0)),
                  pl.BlockSpec((te, D_MODEL), lambda i, s: (s, 0)),
                  pl.BlockSpec((D_MODEL, te), lambda i, s: (0, s)),
                  group(0), group(0), group(1), group(1), full, full, vec, vec],
        out_specs=pl.BlockSpec((tm, D_MODEL), lambda i, s: (i, 0)),
        out_shape=jax.ShapeDtypeStruct((t, D_MODEL), F32),
        scratch_shapes=[pltpu.VMEM((D_MODEL, tm), F32),
                        pltpu.VMEM((PEER_UNIT, tm), F32), pltpu.VMEM((PEER_UNIT, tm), F32),
                        pltpu.VMEM((PEER_UNIT, tm), BF16), pltpu.VMEM((PEER_UNIT, tm), BF16),
                        pltpu.VMEM((tm, D_MODEL), BF16)],
        compiler_params=_params("parallel", "arbitrary"),
        name="peer_experts",
    )(x, u16, vt16, npass, e0, npass, e0, rank, e1, ln_g, ln_b)


def _pad_lanes(v):
    return jnp.pad(v, (0, LANES - v.shape[0])).reshape(1, LANES)


def _prepare_weights(w_in, ssd_conv_w, ssd_conv_b, ssd_dt_bias, ssd_a_log, ssd_d, ssd_norm_w, ssd_w_out,
                     conf_conv_w, conf_conv_b, conf_ln_g, conf_ln_b, conf_w_out, mem_w_k, mem_w_v, mem_w_o,
                     w_out, ln1_g, ln1_b, peer_w_q, peer_sub_keys, peer_u, peer_v, ln2_g, ln2_b):
    w = w_in[0]
    row = lambda v: v[0].reshape(1, -1)
    return dict(
        w_main=jnp.concatenate([w[:, :S_XBC], w[:, S_DT:]], axis=1).astype(BF16),
        w_dt=jnp.pad(w[:, S_XBC:S_DT], ((0, 0), (0, LANES - SSD_HEADS))).astype(BF16),
        ssd_conv_w=ssd_conv_w[0], ssd_conv_b=row(ssd_conv_b),
        dt_bias=_pad_lanes(ssd_dt_bias[0]), a_log=_pad_lanes(ssd_a_log[0]),
        d_row=jnp.repeat(ssd_d[0], SSD_HEADDIM).reshape(1, SSD_INNER), norm_w=row(ssd_norm_w),
        ssd_w_out=ssd_w_out[0].astype(BF16),
        conf_w=jnp.pad(conf_conv_w[0], ((0, CONF_W_ROWS - CONF_WIDTH), (0, 0))), conf_b=row(conf_conv_b),
        conf_ln_g=row(conf_ln_g), conf_ln_b=row(conf_ln_b), conf_w_out=conf_w_out[0].astype(BF16),
        mem_w_k=mem_w_k[0].astype(BF16), mem_w_v=mem_w_v[0].astype(BF16), mem_w_o=mem_w_o[0].astype(BF16),
        w_out=w_out[0].astype(BF16), ln1_g=row(ln1_g), ln1_b=row(ln1_b),
        peer_w_q=peer_w_q[0].astype(BF16),
        peer_keys=peer_sub_keys[0].reshape(N_SUBKEYS, PEER_NKEYS, PEER_DHALF).astype(BF16),
        peer_u=peer_u[0].astype(BF16), peer_vt=peer_v[0].T.astype(BF16),
        ln2_g=row(ln2_g), ln2_b=row(ln2_b),
    )


def _tokenwise_tail(p, x, proj, ynorm, cact, attn, tm_merge, tm_score, tm_expert):
    x1 = _merge(ynorm, cact, attn, proj, x, p["ssd_w_out"], p["conf_w_out"], p["mem_w_o"], p["w_out"],
                p["ln1_g"], p["ln1_b"], tm_merge)
    npass, e0, rank, e1 = _peer_scores(x1, p["peer_w_q"], p["peer_keys"], tm_score)
    return _peer_experts(x1, p["peer_u"], p["peer_vt"], npass, e0, rank, e1, p["ln2_g"], p["ln2_b"], tm_expert)


def _prompt_layer(p, x_prompt, mem_prompt):
    nb, seq, _ = x_prompt.shape
    x = x_prompt.reshape(nb * seq, D_MODEL)
    proj = _matmul(x, p["w_main"], 2048, 1024, "in_proj_prompt")
    dt_raw = _matmul(x, p["w_dt"], 1024, LANES, "dt_proj_prompt")
    mem = mem_prompt.reshape(nb * MEM_TOKENS, D_MODEL)
    mk = _matmul(mem, p["mem_w_k"], 1024, 1024, "mem_k_proj")
    mv = _matmul(mem, p["mem_w_v"], 1024, 1024, "mem_v_proj")
    xbc_c = _ssd_conv_prompt(proj, p["ssd_conv_w"], p["ssd_conv_b"], nb, seq)
    ynorm, ssm = _ssd_scan_prompt(xbc_c, proj, dt_raw, p["dt_bias"], p["a_log"], p["d_row"], p["norm_w"], nb, seq)
    cact, conf_state = _conf_prompt(proj, p["conf_w"], p["conf_b"], p["conf_ln_g"], p["conf_ln_b"], nb, seq)
    attn = _attn_prompt(proj, mk, mv, nb, seq)
    y = _tokenwise_tail(p, x, proj, ynorm, cact, attn, 256, 256, 512)
    ssd_buf = proj.reshape(nb, seq, PROJ_COLS)[:, seq - (SSD_CONV - 1):, S_Z:S_XBC]
    kv_shape = (nb, MEM_TOKENS, MEM_HEADS, MEM_HEADDIM)
    return (y.reshape(nb, seq, D_MODEL), ssm, ssd_buf, conf_state, mk.reshape(kv_shape), mv.reshape(kv_shape))


def _sample_layer(p, x_sample, state_ssd, state_ssd_conv, state_conf_conv, cache_mem_k, cache_mem_v):
    nb = x_sample.shape[0]
    x = x_sample.reshape(nb, D_MODEL)
    proj = _matmul(x, p["w_main"], nb, 1024, "in_proj_sample")
    dt_raw = _matmul(x, p["w_dt"], nb, LANES, "dt_proj_sample")
    xbc_c = _ssd_conv_step(proj, jnp.swapaxes(state_ssd_conv, 0, 1), p["ssd_conv_w"], p["ssd_conv_b"], nb)
    rows = SSD_HEADS * SSD_HEADDIM
    ssm, y_t = _ssd_step(xbc_c, dt_raw, p["dt_bias"], p["a_log"], state_ssd.reshape(nb, rows, SSD_STATE), nb)
    ynorm = _gated_norm_step(y_t, xbc_c, proj, p["d_row"], p["norm_w"], nb)
    cact, glu = _conf_step(proj, jnp.swapaxes(state_conf_conv, 0, 1), p["conf_w"], p["conf_b"],
                           p["conf_ln_g"], p["conf_ln_b"], nb)
    attn = _attn_step(proj, cache_mem_k.reshape(nb, MEM_TOKENS, 1024), cache_mem_v.reshape(nb, MEM_TOKENS, 1024), nb)
    y = _tokenwise_tail(p, x, proj, ynorm, cact, attn, nb, nb, nb)
    ssd_buf = jnp.concatenate([state_ssd_conv[:, 1:], proj[:, None, S_Z:S_XBC]], axis=1)
    conf_buf = jnp.concatenate([state_conf_conv[:, 1:], glu[:, None, :]], axis=1)
    return (y.reshape(nb, 1, D_MODEL), ssm.reshape(nb, SSD_HEADS, SSD_HEADDIM, SSD_STATE), ssd_buf, conf_buf)


def kernel(x_prompt, x_sample, state_ssd, state_ssd_conv, state_conf_conv, cache_mem_k, cache_mem_v, mem_prompt, w_in, ssd_conv_w, ssd_conv_b, ssd_dt_bias, ssd_a_log, ssd_d, ssd_norm_w, ssd_w_out, conf_conv_w, conf_conv_b, conf_ln_g, conf_ln_b, conf_w_out, mem_w_k, mem_w_v, mem_w_o, w_out, ln1_g, ln1_b, peer_w_q, peer_sub_keys, peer_u, peer_v, ln2_g, ln2_b):
    assert w_in.shape[0] == DEPTH == 1
    p = _prepare_weights(w_in, ssd_conv_w, ssd_conv_b, ssd_dt_bias, ssd_a_log, ssd_d, ssd_norm_w, ssd_w_out,
                         conf_conv_w, conf_conv_b, conf_ln_g, conf_ln_b, conf_w_out, mem_w_k, mem_w_v, mem_w_o,
                         w_out, ln1_g, ln1_b, peer_w_q, peer_sub_keys, peer_u, peer_v, ln2_g, ln2_b)
    yp, ssm_p, sbuf_p, cbuf_p, mk_p, mv_p = _prompt_layer(p, x_prompt, mem_prompt)
    ys, ssm_s, sbuf_s, cbuf_s = _sample_layer(p, x_sample, state_ssd[0], state_ssd_conv[0], state_conf_conv[0],
                                              cache_mem_k[0], cache_mem_v[0])
    return (yp, ys, ssm_p[None], sbuf_p[None], cbuf_p[None], mk_p[None], mv_p[None],
            ssm_s[None], sbuf_s[None], cbuf_s[None])
```

```python
import functools
import math

import jax
import jax.numpy as jnp
from jax import lax
from jax.experimental import pallas as pl
from jax.experimental.pallas import tpu as pltpu

F32 = jnp.float32
BF16 = jnp.bfloat16

D_MODEL = 1024
SSD_INNER = 2048
SSD_HEADDIM = 64
SSD_HEADS = 32
SSD_GROUPS = 4
SSD_STATE = 128
SSD_CONV = 4
SSD_CHUNK = 128
SSD_XBC = 3072
CONF_DIM = 1024
CONF_WIDTH = 31
MEM_TOKENS = 256
MEM_HEADS = 4
MEM_HEADDIM = 256
PEER_HEADS = 8
PEER_NKEYS = 128
PEER_EXPERTS = PEER_NKEYS * PEER_NKEYS
PEER_DHALF = 128
PEER_TOPK = 16
DEPTH = 1
ALPHA = (2.0 * DEPTH) ** 0.25
LN_EPS = 1e-5
S_Z = SSD_INNER
S_XBC = S_Z + SSD_XBC
S_DT = S_XBC + SSD_HEADS

COL_XBC = 2
COL_CONF_A = 5
COL_CONF_B = 6
COL_QMEM = 7
COL_GATES = 8
PROJ_COLS = 11 * 1024

LANES = 128
SUBLANES = 8
VMEM_LIMIT_BYTES = 56 * 1024 * 1024


def _params(*sem):
    return pltpu.CompilerParams(dimension_semantics=sem, vmem_limit_bytes=VMEM_LIMIT_BYTES)


def _sigmoid(x):
    return jax.nn.sigmoid(x)


def _silu(x):
    return x * _sigmoid(x)


def _twice_gelu_erf(x):
    return x * (1.0 + lax.erf(x * (1.0 / math.sqrt(2.0))))


def _layer_norm_rows(x, g, b):
    mu = jnp.mean(x, axis=-1, keepdims=True)
    xc = x - mu
    var = jnp.mean(xc * xc, axis=-1, keepdims=True)
    return xc * lax.rsqrt(var + LN_EPS) * g + b


def _split3_bf16(x):
    hi = x.astype(BF16)
    r1 = x - hi.astype(F32)
    mid = r1.astype(BF16)
    lo = (r1 - mid.astype(F32)).astype(BF16)
    return hi, mid, lo


def _dot(a, b):
    return jnp.dot(a, b, preferred_element_type=F32)


def _dot_nt(a, b):
    return lax.dot_general(a, b, (((1,), (1,)), ((), ())), preferred_element_type=F32)


def _mm_kernel(x_ref, w_ref, o_ref):
    o_ref[...] = _dot(x_ref[...].astype(BF16), w_ref[...]).astype(o_ref.dtype)


def _matmul(x, w, tm, tn, name):
    m, k = x.shape
    n = w.shape[1]
    tm = min(tm, m)
    assert m % tm == 0 and n % tn == 0
    return pl.pallas_call(
        _mm_kernel,
        grid=(m // tm, n // tn),
        in_specs=[pl.BlockSpec((tm, k), lambda i, j: (i, 0)),
                  pl.BlockSpec((k, tn), lambda i, j: (0, j))],
        out_specs=pl.BlockSpec((tm, tn), lambda i, j: (i, j)),
        out_shape=jax.ShapeDtypeStruct((m, n), F32),
        compiler_params=_params("parallel", "parallel"),
        name=name,
    )(x, w)


SSD_CONV_ROWS = 512
CONV_CHUNK = 32


def _causal_conv_rows(src, r0, cols, w_ref, bias_ref, taps, first_off):
    win = -(-(first_off + taps - 1 + CONV_CHUNK) // SUBLANES) * SUBLANES
    window = src[r0:r0 + win, cols]
    acc = jnp.broadcast_to(bias_ref[:, cols], (CONV_CHUNK, LANES))
    for rho in range(SUBLANES):
        ks = [k for k in range(taps) if (first_off + k) % SUBLANES == rho]
        if not ks:
            continue
        shifted = window if rho == 0 else pltpu.roll(window, win - rho, 0)
        for k in ks:
            a = (first_off + k) // SUBLANES * SUBLANES
            acc = acc + w_ref[k:k + 1, cols] * shifted[a:a + CONV_CHUNK, :]
    return acc


def _ssd_conv_kernel(x_ref, halo_ref, w_ref, b_ref, o_ref, scr):
    tl = x_ref.shape[0]
    first = pl.program_id(1) == 0
    scr[0:SUBLANES, :] = jnp.where(first, 0.0, halo_ref[...])
    scr[SUBLANES:SUBLANES + tl, :] = x_ref[...]
    for r0 in range(0, tl, CONV_CHUNK):
        for c0 in range(0, x_ref.shape[1], LANES):
            cols = slice(c0, c0 + LANES)
            acc = _causal_conv_rows(scr, r0, cols, w_ref, b_ref, SSD_CONV, SUBLANES - (SSD_CONV - 1))
            o_ref[r0:r0 + CONV_CHUNK, cols] = _silu(acc)


def _ssd_conv_prompt(proj, w, b, nb, seq):
    tl = SSD_CONV_ROWS
    nl = seq // tl
    rows_per_halo = tl // SUBLANES
    return pl.pallas_call(
        _ssd_conv_kernel,
        grid=(nb, nl, SSD_XBC // 1024),
        in_specs=[
            pl.BlockSpec((tl, 1024), lambda bi, l, j: (bi * nl + l, COL_XBC + j)),
            pl.BlockSpec((SUBLANES, 1024),
                         lambda bi, l, j: (jnp.maximum((bi * nl + l) * rows_per_halo - 1, 0), COL_XBC + j)),
            pl.BlockSpec((SSD_CONV, 1024), lambda bi, l, j: (0, j)),
            pl.BlockSpec((1, 1024), lambda bi, l, j: (0, j)),
        ],
        out_specs=pl.BlockSpec((tl, 1024), lambda bi, l, j: (bi * nl + l, j)),
        out_shape=jax.ShapeDtypeStruct((nb * seq, SSD_XBC), F32),
        scratch_shapes=[pltpu.VMEM((SUBLANES + tl, 1024), F32)],
        compiler_params=_params("parallel", "parallel", "parallel"),
        name="ssd_conv_prompt",
    )(proj, proj, w, b)


def _softplus(x):
    return jnp.maximum(x, 0.0) + jnp.log1p(jnp.exp(-jnp.abs(x)))


def _ssd_scan_kernel(xc_ref, z_ref, dtraw_ref, dtb_ref, alog_ref, d_ref, nw_ref,
                     y_ref, st_ref, yscr):
    q = SSD_CHUNK
    c = pl.program_id(1)

    @pl.when(c == 0)
    def _():
        st_ref[...] = jnp.zeros_like(st_ref)

    dt = _softplus(dtraw_ref[...] + dtb_ref[...])
    a = -jnp.exp(alog_ref[...])
    da = dt * a
    row = lax.broadcasted_iota(jnp.int32, (q, q), 0)
    col = lax.broadcasted_iota(jnp.int32, (q, q), 1)
    causal = row >= col
    tri = jnp.where(causal, 1.0, 0.0).astype(BF16)
    hi, mid, lo = _split3_bf16(da)
    a_cs = _dot(tri, hi) + _dot(tri, mid) + _dot(tri, lo)
    a_cs_t = a_cs.T
    dt_t = dt.T
    xs = xc_ref[:, 0:SSD_INNER]
    xs_t = xs.T
    heads_per_group = SSD_HEADS // SSD_GROUPS
    rows_per_group = heads_per_group * SSD_HEADDIM
    first_half = lax.broadcasted_iota(jnp.int32, (q, 2 * SSD_HEADDIM), 1) < SSD_HEADDIM
    per_head_rows = lambda vals: jnp.concatenate(
        [jnp.broadcast_to(v, (SSD_HEADDIM, v.shape[1])) for v in vals], axis=0)
    for g in range(SSD_GROUPS):
        heads = range(g * heads_per_group, (g + 1) * heads_per_group)
        rows = slice(g * rows_per_group, (g + 1) * rows_per_group)
        b_g16 = xc_ref[:, SSD_INNER + g * SSD_STATE:SSD_INNER + (g + 1) * SSD_STATE].astype(BF16)
        c_g16 = xc_ref[:, SSD_INNER + (SSD_GROUPS + g) * SSD_STATE:
                       SSD_INNER + (SSD_GROUPS + g + 1) * SSD_STATE].astype(BF16)
        cb = _dot_nt(c_g16, b_g16)
        st_g = st_ref[0, rows, :]
        y_off = _dot_nt(c_g16, st_g.astype(BF16))
        last = [a_cs_t[h:h + 1, q - 1:q] for h in heads]
        w_rows = per_head_rows([jnp.exp(last[i] - a_cs_t[h:h + 1, :]) * dt_t[h:h + 1, :]
                                for i, h in enumerate(heads)])
        states = _dot((xs_t[rows, :] * w_rows).astype(BF16), b_g16)
        st_ref[0, rows, :] = st_g * per_head_rows([jnp.exp(v) * jnp.ones((1, SSD_STATE), F32) for v in last]) + states
        for r in range(0, heads_per_group, 2):
            h0 = g * heads_per_group + r
            m1, ea = [], []
            for h in (h0, h0 + 1):
                acs_col = a_cs[:, h:h + 1]
                seg = jnp.where(causal, acs_col - a_cs_t[h:h + 1, :], -1e30)
                m1.append((cb * jnp.exp(seg) * dt_t[h:h + 1, :]).astype(BF16))
                ea.append(jnp.broadcast_to(jnp.exp(acs_col), (q, SSD_HEADDIM)))
            cols = slice(h0 * SSD_HEADDIM, (h0 + 2) * SSD_HEADDIM)
            x_pair = xs[:, cols]
            x_diag = jnp.concatenate([jnp.where(first_half, x_pair, 0.0), jnp.where(first_half, 0.0, x_pair)],
                                     axis=0).astype(BF16)
            off_cols = slice(r * SSD_HEADDIM, (r + 2) * SSD_HEADDIM)
            yscr[:, cols] = (_dot(jnp.concatenate(m1, axis=1), x_diag)
                             + y_off[:, off_cols] * jnp.concatenate(ea, axis=1))
    y = yscr[...] + xs * d_ref[...]
    hgate = y * _silu(z_ref[...])
    ms = jnp.mean(hgate * hgate, axis=-1, keepdims=True)
    y_ref[...] = (hgate * lax.rsqrt(ms + LN_EPS) * nw_ref[...]).astype(y_ref.dtype)


def _ssd_scan_prompt(xbc_c, proj, dt_raw, dt_bias, a_log, d_row, norm_w, nb, seq):
    q = SSD_CHUNK
    nc = seq // q
    return pl.pallas_call(
        _ssd_scan_kernel,
        grid=(nb, nc),
        in_specs=[
            pl.BlockSpec((q, SSD_XBC), lambda bi, c: (bi * nc + c, 0)),
            pl.BlockSpec((q, SSD_INNER), lambda bi, c: (bi * nc + c, 0)),
            pl.BlockSpec((q, LANES), lambda bi, c: (bi * nc + c, 0)),
            pl.BlockSpec((1, LANES), lambda bi, c: (0, 0)),
            pl.BlockSpec((1, LANES), lambda bi, c: (0, 0)),
            pl.BlockSpec((1, SSD_INNER), lambda bi, c: (0, 0)),
            pl.BlockSpec((1, SSD_INNER), lambda bi, c: (0, 0)),
        ],
        out_specs=[
            pl.BlockSpec((q, SSD_INNER), lambda bi, c: (bi * nc + c, 0)),
            pl.BlockSpec((1, SSD_HEADS * SSD_HEADDIM, SSD_STATE), lambda bi, c: (bi, 0, 0)),
        ],
        out_shape=[jax.ShapeDtypeStruct((nb * seq, SSD_INNER), BF16),
                   jax.ShapeDtypeStruct((nb, SSD_HEADS * SSD_HEADDIM, SSD_STATE), F32)],
        scratch_shapes=[pltpu.VMEM((q, SSD_INNER), F32)],
        compiler_params=_params("parallel", "arbitrary"),
        name="ssd_scan_prompt",
    )(xbc_c, proj, dt_raw, dt_bias, a_log, d_row, norm_w)


def _ssd_conv_step_kernel(x_ref, st_ref, w_ref, b_ref, o_ref):
    acc = b_ref[...] + w_ref[SSD_CONV - 1:SSD_CONV, :] * x_ref[...]
    for k in range(SSD_CONV - 1):
        acc = acc + w_ref[k:k + 1, :] * st_ref[k]
    o_ref[...] = _silu(acc)


def _ssd_conv_step(proj, state_t, w, b, nb):
    return pl.pallas_call(
        _ssd_conv_step_kernel,
        grid=(SSD_XBC // 1024,),
        in_specs=[
            pl.BlockSpec((nb, 1024), lambda j: (0, COL_XBC + j)),
            pl.BlockSpec((SSD_CONV - 1, nb, 1024), lambda j: (0, 0, j)),
            pl.BlockSpec((SSD_CONV, 1024), lambda j: (0, j)),
            pl.BlockSpec((1, 1024), lambda j: (0, j)),
        ],
        out_specs=pl.BlockSpec((nb, 1024), lambda j: (0, j)),
        out_shape=jax.ShapeDtypeStruct((nb, SSD_XBC), F32),
        compiler_params=_params("parallel"),
        name="ssd_conv_step",
    )(proj, state_t, w, b)


def _ssd_step_kernel(xc_ref, bc_ref, dtraw_ref, dtb_ref, alog_ref, st_ref, st_out_ref, yt_ref, xdt_scr, dec_scr):
    nb = xc_ref.shape[0]
    bi = pl.program_id(0)
    rows = SSD_HEADS * SSD_HEADDIM

    @pl.when(bi == 0)
    def _():
        dt = _softplus(dtraw_ref[...] + dtb_ref[...])
        a = -jnp.exp(alog_ref[...])
        dt_t = dt.T
        da_t = (dt * a).T
        expand = lambda t: jnp.concatenate(
            [jnp.broadcast_to(t[h:h + 1, :], (SSD_HEADDIM, nb)) for h in range(SSD_HEADS)], axis=0)
        xs_t = xc_ref[:, 0:SSD_INNER].T
        xdt_scr[...] = xs_t * expand(dt_t)
        dec_scr[...] = jnp.exp(expand(da_t))
        yt_ref[...] = jnp.zeros_like(yt_ref)

    lane_b = lax.broadcasted_iota(jnp.int32, (nb, LANES), 0)
    onehot = jnp.where(lane_b == bi, 1.0, 0.0).astype(BF16)

    def pick(ref):
        hi, mid, lo = _split3_bf16(ref[...])
        return _dot(hi, onehot) + _dot(mid, onehot) + _dot(lo, onehot)

    xdt_b = pick(xdt_scr)
    dec_b = pick(dec_scr)
    out_lane = lax.broadcasted_iota(jnp.int32, (rows // SSD_GROUPS, LANES), 1) == bi
    for g in range(SSD_GROUPS):
        r0 = g * (rows // SSD_GROUPS)
        r1 = r0 + rows // SSD_GROUPS
        b_row = bc_ref[0, :, g * SSD_STATE:(g + 1) * SSD_STATE]
        c_row = bc_ref[0, :, (SSD_GROUPS + g) * SSD_STATE:(SSD_GROUPS + g + 1) * SSD_STATE]
        h_new = st_ref[0, r0:r1, :] * dec_b[r0:r1, :] + xdt_b[r0:r1, :] * b_row
        st_out_ref[0, r0:r1, :] = h_new
        y_col = jnp.sum(h_new * c_row, axis=-1, keepdims=True)
        yt_ref[r0:r1, :] = jnp.where(out_lane, y_col, yt_ref[r0:r1, :])


def _ssd_step(xbc_c, dt_raw, dt_bias, a_log, state, nb):
    rows = SSD_HEADS * SSD_HEADDIM
    assert nb == LANES
    bc_rows = xbc_c[:, SSD_INNER:].reshape(nb, 1, SSD_XBC - SSD_INNER)
    return pl.pallas_call(
        _ssd_step_kernel,
        grid=(nb,),
        in_specs=[
            pl.BlockSpec((nb, SSD_XBC), lambda bi: (0, 0)),
            pl.BlockSpec((1, 1, bc_rows.shape[2]), lambda bi: (bi, 0, 0)),
            pl.BlockSpec((nb, LANES), lambda bi: (0, 0)),
            pl.BlockSpec((1, LANES), lambda bi: (0, 0)),
            pl.BlockSpec((1, LANES), lambda bi: (0, 0)),
            pl.BlockSpec((1, rows, SSD_STATE), lambda bi: (bi, 0, 0)),
        ],
        out_specs=[
            pl.BlockSpec((1, rows, SSD_STATE), lambda bi: (bi, 0, 0)),
            pl.BlockSpec((rows, nb), lambda bi: (0, 0)),
        ],
        out_shape=[jax.ShapeDtypeStruct((nb, rows, SSD_STATE), F32),
                   jax.ShapeDtypeStruct((rows, nb), F32)],
        scratch_shapes=[pltpu.VMEM((rows, nb), F32), pltpu.VMEM((rows, nb), F32)],
        compiler_params=_params("arbitrary"),
        name="ssd_step",
    )(xbc_c, bc_rows, dt_raw, dt_bias, a_log, state)


def _gated_norm_step_kernel(yt_ref, xc_ref, z_ref, d_ref, nw_ref, o_ref):
    y = yt_ref[...].T + xc_ref[:, 0:SSD_INNER] * d_ref[...]
    hgate = y * _silu(z_ref[...])
    ms = jnp.mean(hgate * hgate, axis=-1, keepdims=True)
    o_ref[...] = (hgate * lax.rsqrt(ms + LN_EPS) * nw_ref[...]).astype(o_ref.dtype)


def _gated_norm_step(y_t, xbc_c, proj, d_row, norm_w, nb):
    return pl.pallas_call(
        _gated_norm_step_kernel,
        grid=(1,),
        in_specs=[
            pl.BlockSpec((SSD_INNER, nb), lambda i: (0, 0)),
            pl.BlockSpec((nb, SSD_XBC), lambda i: (0, 0)),
            pl.BlockSpec((nb, SSD_INNER), lambda i: (0, 0)),
            pl.BlockSpec((1, SSD_INNER), lambda i: (0, 0)),
            pl.BlockSpec((1, SSD_INNER), lambda i: (0, 0)),
        ],
        out_specs=pl.BlockSpec((nb, SSD_INNER), lambda i: (0, 0)),
        out_shape=jax.ShapeDtypeStruct((nb, SSD_INNER), BF16),
        compiler_params=_params("arbitrary"),
        name="ssd_gated_norm_step",
    )(y_t, xbc_c, proj, d_row, norm_w)


CONF_ROWS = 256
CONF_PAD = 32
CONF_W_ROWS = 32


def _conf_kernel(a_ref, b_ref, w_ref, cb_ref, g_ref, beta_ref, o_ref, st_ref, gl, cv):
    tl = a_ref.shape[0]
    l = pl.program_id(1)

    @pl.when(l == 0)
    def _():
        gl[0:CONF_PAD, :] = jnp.zeros((CONF_PAD, CONF_DIM), F32)

    gl[CONF_PAD:CONF_PAD + tl, :] = a_ref[...] * _sigmoid(b_ref[...])
    for r0 in range(0, tl, CONV_CHUNK):
        for c0 in range(0, CONF_DIM, LANES):
            cols = slice(c0, c0 + LANES)
            cv[r0:r0 + CONV_CHUNK, cols] = _causal_conv_rows(gl, r0, cols, w_ref, cb_ref, CONF_WIDTH,
                                                             CONF_PAD - (CONF_WIDTH - 1))
    o_ref[...] = _silu(_layer_norm_rows(cv[...], g_ref[...], beta_ref[...])).astype(o_ref.dtype)

    @pl.when(l == pl.num_programs(1) - 1)
    def _():
        st_ref[0] = gl[CONF_PAD + tl - (CONF_WIDTH - 1):CONF_PAD + tl, :]

    gl[0:CONF_PAD, :] = gl[tl:tl + CONF_PAD, :]


def _conf_prompt(proj, w, cb, ln_g, ln_b, nb, seq):
    tl = CONF_ROWS
    nl = seq // tl
    vec = pl.BlockSpec((1, CONF_DIM), lambda bi, l: (0, 0))
    return pl.pallas_call(
        _conf_kernel,
        grid=(nb, nl),
        in_specs=[
            pl.BlockSpec((tl, 1024), lambda bi, l: (bi * nl + l, COL_CONF_A)),
            pl.BlockSpec((tl, 1024), lambda bi, l: (bi * nl + l, COL_CONF_B)),
            pl.BlockSpec((CONF_W_ROWS, CONF_DIM), lambda bi, l: (0, 0)),
            vec, vec, vec,
        ],
        out_specs=[
            pl.BlockSpec((tl, CONF_DIM), lambda bi, l: (bi * nl + l, 0)),
            pl.BlockSpec((1, CONF_WIDTH - 1, CONF_DIM), lambda bi, l: (bi, 0, 0)),
        ],
        out_shape=[jax.ShapeDtypeStruct((nb * seq, CONF_DIM), BF16),
                   jax.ShapeDtypeStruct((nb, CONF_WIDTH - 1, CONF_DIM), F32)],
        scratch_shapes=[pltpu.VMEM((CONF_PAD + tl, CONF_DIM), F32), pltpu.VMEM((tl, CONF_DIM), F32)],
        compiler_params=_params("parallel", "arbitrary"),
        name="conf_prompt",
    )(proj, proj, w, cb, ln_g, ln_b)


def _conf_step_kernel(a_ref, b_ref, st_ref, w_ref, cb_ref, g_ref, beta_ref, o_ref, glu_ref):
    glu = a_ref[...] * _sigmoid(b_ref[...])
    glu_ref[...] = glu
    acc = cb_ref[...] + w_ref[CONF_WIDTH - 1:CONF_WIDTH, :] * glu
    for k in range(CONF_WIDTH - 1):
        acc = acc + w_ref[k:k + 1, :] * st_ref[k]
    o_ref[...] = _silu(_layer_norm_rows(acc, g_ref[...], beta_ref[...])).astype(o_ref.dtype)


def _conf_step(proj, state_t, w, cb, ln_g, ln_b, nb):
    vec = pl.BlockSpec((1, CONF_DIM), lambda i: (0, 0))
    return pl.pallas_call(
        _conf_step_kernel,
        grid=(1,),
        in_specs=[
            pl.BlockSpec((nb, 1024), lambda i: (0, COL_CONF_A)),
            pl.BlockSpec((nb, 1024), lambda i: (0, COL_CONF_B)),
            pl.BlockSpec((CONF_WIDTH - 1, nb, CONF_DIM), lambda i: (0, 0, 0)),
            pl.BlockSpec((CONF_W_ROWS, CONF_DIM), lambda i: (0, 0)),
            vec, vec, vec,
        ],
        out_specs=[pl.BlockSpec((nb, CONF_DIM), lambda i: (0, 0)),
                   pl.BlockSpec((nb, CONF_DIM), lambda i: (0, 0))],
        out_shape=[jax.ShapeDtypeStruct((nb, CONF_DIM), BF16),
                   jax.ShapeDtypeStruct((nb, CONF_DIM), F32)],
        compiler_params=_params("arbitrary"),
        name="conf_step",
    )(proj, proj, state_t, w, cb, ln_g, ln_b)


ATTN_ROWS = 512
MEM_SCALE = MEM_HEADDIM ** -0.5


def _softmax_lanes(sc):
    mx = jnp.max(sc, axis=-1, keepdims=True)
    e = jnp.exp(sc - mx)
    return e / jnp.sum(e, axis=-1, keepdims=True)


def _attn_kernel(q_ref, k_ref, v_ref, o_ref):
    for h in range(MEM_HEADS):
        sl = slice(h * MEM_HEADDIM, (h + 1) * MEM_HEADDIM)
        sc = _dot_nt(q_ref[:, sl].astype(BF16), k_ref[:, sl].astype(BF16)) * MEM_SCALE
        pr = _softmax_lanes(sc).astype(BF16)
        o_ref[:, sl] = _dot(pr, v_ref[:, sl].astype(BF16)).astype(o_ref.dtype)


def _attn_prompt(proj, mk, mv, nb, seq):
    tq = ATTN_ROWS
    nq = seq // tq
    kv = pl.BlockSpec((MEM_TOKENS, 1024), lambda bi, i: (bi, 0))
    return pl.pallas_call(
        _attn_kernel,
        grid=(nb, nq),
        in_specs=[pl.BlockSpec((tq, 1024), lambda bi, i: (bi * nq + i, COL_QMEM)), kv, kv],
        out_specs=pl.BlockSpec((tq, 1024), lambda bi, i: (bi * nq + i, 0)),
        out_shape=jax.ShapeDtypeStruct((nb * seq, 1024), BF16),
        compiler_params=_params("parallel", "parallel"),
        name="attn_prompt",
    )(proj, mk, mv)


ATTN_STEP_BATCH = 2


def _attn_step_kernel(q_ref, k_ref, v_ref, o_ref):
    for i in range(ATTN_STEP_BATCH):
        q = q_ref[i]
        sc = jnp.sum(k_ref[i] * q[None], axis=-1, keepdims=True) * MEM_SCALE
        e = jnp.exp(sc - jnp.max(sc, axis=0, keepdims=True))
        pr = e / jnp.sum(e, axis=0, keepdims=True)
        o_ref[i] = jnp.sum(pr * v_ref[i], axis=0)


def _attn_step(proj, mem_k, mem_v, nb):
    bb = ATTN_STEP_BATCH
    q = proj[:, COL_QMEM * 1024:(COL_QMEM + 1) * 1024].reshape(nb, MEM_HEADS, MEM_HEADDIM)
    kv = pl.BlockSpec((bb, MEM_TOKENS, MEM_HEADS, MEM_HEADDIM), lambda i: (i, 0, 0, 0))
    qo = pl.BlockSpec((bb, MEM_HEADS, MEM_HEADDIM), lambda i: (i, 0, 0))
    out = pl.pallas_call(
        _attn_step_kernel,
        grid=(nb // bb,),
        in_specs=[qo, kv, kv],
        out_specs=qo,
        out_shape=jax.ShapeDtypeStruct((nb, MEM_HEADS, MEM_HEADDIM), F32),
        compiler_params=_params("parallel"),
        name="attn_step",
    )(q, mem_k, mem_v)
    return out.reshape(nb, MEM_HEADS * MEM_HEADDIM)


def _merge_kernel(yn_ref, ca_ref, at_ref, ga_ref, gb_ref, gc_ref, x_ref, wa_ref, wb_ref, wc_ref, wo_ref,
                  g_ref, b_ref, o_ref):
    br_a = _dot(yn_ref[...].astype(BF16), wa_ref[...])
    br_b = _dot(ca_ref[...].astype(BF16), wb_ref[...])
    br_c = _dot(at_ref[...].astype(BF16), wc_ref[...])
    merged = _sigmoid(ga_ref[...]) * br_a + _sigmoid(gb_ref[...]) * br_b + _sigmoid(gc_ref[...]) * br_c
    res = ALPHA * x_ref[...] + _dot(merged.astype(BF16), wo_ref[...])
    o_ref[...] = _layer_norm_rows(res, g_ref[...], b_ref[...])


def _merge(ynorm, cact, attn, proj, x, w_a, w_b, w_c, w_o, ln_g, ln_b, tm):
    t = x.shape[0]
    row = lambda w: pl.BlockSpec((tm, w), lambda i: (i, 0))
    gate = lambda j: pl.BlockSpec((tm, D_MODEL), lambda i: (i, COL_GATES + j))
    full = lambda a: pl.BlockSpec(a.shape, lambda i: (0, 0))
    return pl.pallas_call(
        _merge_kernel,
        grid=(t // tm,),
        in_specs=[row(SSD_INNER), row(CONF_DIM), row(1024), gate(0), gate(1), gate(2),
                  row(D_MODEL), full(w_a), full(w_b), full(w_c), full(w_o), full(ln_g), full(ln_b)],
        out_specs=row(D_MODEL),
        out_shape=jax.ShapeDtypeStruct((t, D_MODEL), F32),
        compiler_params=_params("parallel"),
        name="merge_ln1",
    )(ynorm, cact, attn, proj, proj, proj, x, w_a, w_b, w_c, w_o, ln_g, ln_b)


N_SUBKEYS = 2 * PEER_HEADS
STAIRCASE = [(i, j) for i in range(PEER_TOPK) for j in range(PEER_TOPK) if (i + 1) * (j + 1) <= PEER_TOPK]


def _compare_exchange(v, i, j):
    a, b = v[i], v[j]
    v[i] = jnp.maximum(a, b)
    v[j] = jnp.minimum(a, b)


def _bitonic_merge_desc(v):
    n = len(v)
    j = n // 2
    while j >= 1:
        for i in range(n):
            if i ^ j > i:
                _compare_exchange(v, i, i ^ j)
        j //= 2


def _bitonic_sort_desc(v):
    n = len(v)
    k = 2
    while k <= n:
        j = k // 2
        while j >= 1:
            for i in range(n):
                l = i ^ j
                if l > i:
                    if i & k == 0:
                        _compare_exchange(v, i, l)
                    else:
                        _compare_exchange(v, l, i)
            j //= 2
        k *= 2


def _top16_sorted(s_t):
    v = [s_t[r * SUBLANES:(r + 1) * SUBLANES, :] for r in range(PEER_NKEYS // SUBLANES)]
    _bitonic_sort_desc(v)
    for shift in (4, 2, 1):
        other = [pltpu.roll(x, shift, 0) for x in v]
        v = [jnp.maximum(v[i], other[PEER_TOPK - 1 - i]) for i in range(PEER_TOPK)]
        _bitonic_merge_desc(v)
    return v


def _prefix_length(test, vals):
    w = jnp.where
    c8 = test(vals[7])
    c4 = test(w(c8, vals[11], vals[3]))
    c2 = test(w(c8, w(c4, vals[13], vals[9]), w(c4, vals[5], vals[1])))
    c1 = test(w(c8, w(c4, w(c2, vals[14], vals[12]), w(c2, vals[10], vals[8])),
                w(c4, w(c2, vals[6], vals[4]), w(c2, vals[2], vals[0]))))
    c16 = test(vals[15])
    return w(c8, 8.0, 0.0) + w(c4, 4.0, 0.0) + w(c2, 2.0, 0.0) + w(c1, 1.0, 0.0) + w(c16, 1.0, 0.0)


def _pack_factor():
    return 4 // jnp.dtype(BF16).itemsize


def _pack_rows(x):
    return pltpu.bitcast(x.astype(BF16), jnp.uint32)


def _unpack_rows(w):
    return pltpu.bitcast(w, BF16)


def _peer_score_kernel(x_ref, wq_ref, keys_ref, npass_ref, e0_ref, rank_ref, e1_ref, qv_scr, top_scr, s1_scr):
    tm = x_ref.shape[0]
    qv_scr[...] = _dot(x_ref[...].astype(BF16), wq_ref[...]).astype(BF16)
    for hk in range(N_SUBKEYS):
        q_hk = qv_scr[:, hk * PEER_DHALF:(hk + 1) * PEER_DHALF]
        s_t = _dot_nt(keys_ref[hk], q_hk)
        if hk % 2 == 0:
            npass_ref[hk // 2] = s_t
        else:
            s1_scr[hk // 2] = s_t
        top = _top16_sorted(s_t)
        for i in range(PEER_TOPK):
            top_scr[hk % 2, i, hk // 2:hk // 2 + 1, :] = top[i][0:1, :]
    a = [top_scr[0, i] for i in range(PEER_TOPK)]
    b = [top_scr[1, i] for i in range(PEER_TOPK)]
    cand = [a[i] + b[j] for i, j in STAIRCASE]
    ordered = cand + [jnp.full((PEER_HEADS, tm), -jnp.inf, F32)] * (64 - len(cand))
    _bitonic_sort_desc(ordered)
    tau = ordered[PEER_TOPK - 1]
    top_sum = a[0] + b[0]
    z = jnp.zeros((PEER_HEADS, tm), F32)
    for c in cand:
        z = z + jnp.where(c >= tau, jnp.exp(c - top_sum), 0.0)
    half_inv_z = 0.5 / z
    for h in range(PEER_HEADS):
        s0 = npass_ref[h]
        s1 = s1_scr[h]
        e0_ref[h] = jnp.exp(s0 - a[0][h:h + 1, :])
        e1_ref[h] = _pack_rows(jnp.exp(s1 - b[0][h:h + 1, :]) * half_inv_z[h:h + 1, :])
        b_rows = [b[r][h:h + 1, :] for r in range(PEER_TOPK)]
        tau_h = tau[h:h + 1, :]
        npass_ref[h] = _prefix_length(lambda v: s0 + v >= tau_h, b_rows)
        rank_ref[h] = _pack_rows(_prefix_length(lambda v: v > s1, b_rows))


def _peer_scores(x, w_q, keys, tm):
    t = x.shape[0]
    spec = pl.BlockSpec((PEER_HEADS, PEER_NKEYS, tm), lambda i: (0, 0, i))
    spec16 = pl.BlockSpec((PEER_HEADS, PEER_NKEYS // _pack_factor(), tm), lambda i: (0, 0, i))
    shape = jax.ShapeDtypeStruct((PEER_HEADS, PEER_NKEYS, t), F32)
    shape16 = jax.ShapeDtypeStruct((PEER_HEADS, PEER_NKEYS // _pack_factor(), t), jnp.uint32)
    return pl.pallas_call(
        _peer_score_kernel,
        grid=(t // tm,),
        in_specs=[pl.BlockSpec((tm, D_MODEL), lambda i: (i, 0)),
                  pl.BlockSpec(w_q.shape, lambda i: (0, 0)),
                  pl.BlockSpec(keys.shape, lambda i: (0, 0, 0))],
        out_specs=[spec, spec, spec16, spec16],
        out_shape=[shape, shape, shape16, shape16],
        scratch_shapes=[pltpu.VMEM((tm, N_SUBKEYS * PEER_DHALF), BF16),
                        pltpu.VMEM((2, PEER_TOPK, PEER_HEADS, tm), F32),
                        pltpu.VMEM((PEER_HEADS, PEER_NKEYS, tm), F32)],
        compiler_params=_params("parallel"),
        name="peer_scores",
    )(x, w_q, keys)


PEER_UNIT = 1024
I_PER_UNIT = PEER_UNIT // PEER_NKEYS
assert I_PER_UNIT == SUBLANES
UNITS = PEER_EXPERTS // PEER_UNIT
UNITS_PER_STEP = 2
II_GROUP = 2


def _peer_gate_chunk(act_ref, h_ref, npass_ref, e0_ref, rank_ref, e1_ref, c0, ii0):
    gate_sub = SUBLANES * _pack_factor()
    cols = slice(c0, c0 + LANES)
    sub = (gate_sub, LANES)
    zero = jnp.zeros(sub, BF16)
    row = lambda ref, h, ii: jnp.broadcast_to(ref[h, ii:ii + 1, cols], sub).astype(BF16)
    npass = [[row(npass_ref, h, ii0 + q) for h in range(PEER_HEADS)] for q in range(II_GROUP)]
    e0 = [[row(e0_ref, h, ii0 + q) for h in range(PEER_HEADS)] for q in range(II_GROUP)]
    for j0 in range(0, PEER_NKEYS, gate_sub):
        words = slice(j0 // _pack_factor(), j0 // _pack_factor() + SUBLANES)
        gate = [zero] * II_GROUP
        for h in range(PEER_HEADS):
            rank = _unpack_rows(rank_ref[h, words, cols])
            e1 = _unpack_rows(e1_ref[h, words, cols])
            for q in range(II_GROUP):
                gate[q] = gate[q] + e0[q][h] * jnp.where(rank < npass[q][h], e1, zero)
        for q in range(II_GROUP):
            r0 = (ii0 + q) * PEER_NKEYS + j0
            h_ref[r0:r0 + gate_sub, cols] = gate[q] * _twice_gelu_erf(act_ref[r0:r0 + gate_sub, cols]).astype(BF16)


def _peer_unit(tm, u_ref, vt_ref, off, npass_ref, e0_ref, rank_ref, e1_ref, xb, act, h, acc_t):
    act[...] = _dot_nt(u_ref[off:off + PEER_UNIT, :], xb[...])
    for c0 in range(0, tm, LANES):
        for ii in range(0, I_PER_UNIT, II_GROUP):
            _peer_gate_chunk(act, h, npass_ref, e0_ref, rank_ref, e1_ref, c0, ii)
    acc_t[...] += _dot(vt_ref[:, off:off + PEER_UNIT], h[...])


def _peer_expert_kernel(x_ref, u_ref, vt_ref, np_a, e0_a, np_b, e0_b, rank_ref, e1_ref, g_ref, b_ref, o_ref,
                        acc_t, act0, act1, h0, h1, xb):
    tm = x_ref.shape[0]
    step = pl.program_id(1)

    @pl.when(step == 0)
    def _():
        acc_t[...] = jnp.zeros_like(acc_t)
        xb[...] = x_ref[...].astype(BF16)

    _peer_unit(tm, u_ref, vt_ref, 0, np_a, e0_a, rank_ref, e1_ref, xb, act0, h0, acc_t)
    _peer_unit(tm, u_ref, vt_ref, PEER_UNIT, np_b, e0_b, rank_ref, e1_ref, xb, act1, h1, acc_t)

    @pl.when(step == pl.num_programs(1) - 1)
    def _():
        res = ALPHA * x_ref[...] + acc_t[...].T
        o_ref[...] = _layer_norm_rows(res, g_ref[...], b_ref[...])


def _peer_experts(x, u16, vt16, npass, e0, rank, e1, ln_g, ln_b, tm):
    t = x.shape[0]
    te = PEER_UNIT * UNITS_PER_STEP
    group = lambda q: pl.BlockSpec((PEER_HEADS, SUBLANES, tm), lambda i, s: (0, UNITS_PER_STEP * s + q, i))
    full = pl.BlockSpec((PEER_HEADS, PEER_NKEYS // _pack_factor(), tm), lambda i, s: (0, 0, i))
    vec = pl.BlockSpec((1, D_MODEL), lambda i, s: (0, 0))
    return pl.pallas_call(
        _peer_expert_kernel,
        grid=(t // tm, UNITS // UNITS_PER_STEP),
        in_specs=[pl.BlockSpec((tm, D_MODEL), lambda i, s: (i, 0)),
                  pl.BlockSpec((te, D_MODEL), lambda i, s: (s, 0)),
                  pl.BlockSpec((D_MODEL, te), lambda i, s: (0, s)),
                  group(0), group(0), group(1), group(1), full, full, vec, vec],
        out_specs=pl.BlockSpec((tm, D_MODEL), lambda i, s: (i, 0)),
        out_shape=jax.ShapeDtypeStruct((t, D_MODEL), F32),
        scratch_shapes=[pltpu.VMEM((D_MODEL, tm), F32),
                        pltpu.VMEM((PEER_UNIT, tm), F32), pltpu.VMEM((PEER_UNIT, tm), F32),
                        pltpu.VMEM((PEER_UNIT, tm), BF16), pltpu.VMEM((PEER_UNIT, tm), BF16),
                        pltpu.VMEM((tm, D_MODEL), BF16)],
        compiler_params=_params("parallel", "arbitrary"),
        name="peer_experts",
    )(x, u16, vt16, npass, e0, npass, e0, rank, e1, ln_g, ln_b)


def _pad_lanes(v):
    return jnp.pad(v, (0, LANES - v.shape[0])).reshape(1, LANES)


def _prepare_weights(w_in, ssd_conv_w, ssd_conv_b, ssd_dt_bias, ssd_a_log, ssd_d, ssd_norm_w, ssd_w_out,
                     conf_conv_w, conf_conv_b, conf_ln_g, conf_ln_b, conf_w_out, mem_w_k, mem_w_v, mem_w_o,
                     w_out, ln1_g, ln1_b, peer_w_q, peer_sub_keys, peer_u, peer_v, ln2_g, ln2_b):
    w = w_in[0]
    row = lambda v: v[0].reshape(1, -1)
    return dict(
        w_main=jnp.concatenate([w[:, :S_XBC], w[:, S_DT:]], axis=1).astype(BF16),
        w_dt=jnp.pad(w[:, S_XBC:S_DT], ((0, 0), (0, LANES - SSD_HEADS))).astype(BF16),
        ssd_conv_w=ssd_conv_w[0], ssd_conv_b=row(ssd_conv_b),
        dt_bias=_pad_lanes(ssd_dt_bias[0]), a_log=_pad_lanes(ssd_a_log[0]),
        d_row=jnp.repeat(ssd_d[0], SSD_HEADDIM).reshape(1, SSD_INNER), norm_w=row(ssd_norm_w),
        ssd_w_out=ssd_w_out[0].astype(BF16),
        conf_w=jnp.pad(conf_conv_w[0], ((0, CONF_W_ROWS - CONF_WIDTH), (0, 0))), conf_b=row(conf_conv_b),
        conf_ln_g=row(conf_ln_g), conf_ln_b=row(conf_ln_b), conf_w_out=conf_w_out[0].astype(BF16),
        mem_w_k=mem_w_k[0].astype(BF16), mem_w_v=mem_w_v[0].astype(BF16), mem_w_o=mem_w_o[0].astype(BF16),
        w_out=w_out[0].astype(BF16), ln1_g=row(ln1_g), ln1_b=row(ln1_b),
        peer_w_q=peer_w_q[0].astype(BF16),
        peer_keys=peer_sub_keys[0].reshape(N_SUBKEYS, PEER_NKEYS, PEER_DHALF).astype(BF16),
        peer_u=peer_u[0].astype(BF16), peer_vt=peer_v[0].T.astype(BF16),
        ln2_g=row(ln2_g), ln2_b=row(ln2_b),
    )


def _tokenwise_tail(p, x, proj, ynorm, cact, attn, tm_merge, tm_score, tm_expert):
    x1 = _merge(ynorm, cact, attn, proj, x, p["ssd_w_out"], p["conf_w_out"], p["mem_w_o"], p["w_out"],
                p["ln1_g"], p["ln1_b"], tm_merge)
    npass, e0, rank, e1 = _peer_scores(x1, p["peer_w_q"], p["peer_keys"], tm_score)
    return _peer_experts(x1, p["peer_u"], p["peer_vt"], npass, e0, rank, e1, p["ln2_g"], p["ln2_b"], tm_expert)


def _prompt_layer(p, x_prompt, mem_prompt):
    nb, seq, _ = x_prompt.shape
    x = x_prompt.reshape(nb * seq, D_MODEL)
    proj = _matmul(x, p["w_main"], 2048, 1024, "in_proj_prompt")
    dt_raw = _matmul(x, p["w_dt"], 1024, LANES, "dt_proj_prompt")
    mem = mem_prompt.reshape(nb * MEM_TOKENS, D_MODEL)
    mk = _matmul(mem, p["mem_w_k"], 1024, 1024, "mem_k_proj")
    mv = _matmul(mem, p["mem_w_v"], 1024, 1024, "mem_v_proj")
    xbc_c = _ssd_conv_prompt(proj, p["ssd_conv_w"], p["ssd_conv_b"], nb, seq)
    ynorm, ssm = _ssd_scan_prompt(xbc_c, proj, dt_raw, p["dt_bias"], p["a_log"], p["d_row"], p["norm_w"], nb, seq)
    cact, conf_state = _conf_prompt(proj, p["conf_w"], p["conf_b"], p["conf_ln_g"], p["conf_ln_b"], nb, seq)
    attn = _attn_prompt(proj, mk, mv, nb, seq)
    y = _tokenwise_tail(p, x, proj, ynorm, cact, attn, 256, 256, 512)
    ssd_buf = proj.reshape(nb, seq, PROJ_COLS)[:, seq - (SSD_CONV - 1):, S_Z:S_XBC]
    kv_shape = (nb, MEM_TOKENS, MEM_HEADS, MEM_HEADDIM)
    ssm = ssm.reshape(nb, SSD_HEADS, SSD_HEADDIM, SSD_STATE)
    return (y.reshape(nb, seq, D_MODEL), ssm, ssd_buf, conf_state, mk.reshape(kv_shape), mv.reshape(kv_shape))


def _sample_layer(p, x_sample, state_ssd, state_ssd_conv, state_conf_conv, cache_mem_k, cache_mem_v):
    nb = x_sample.shape[0]
    x = x_sample.reshape(nb, D_MODEL)
    proj = _matmul(x, p["w_main"], nb, 1024, "in_proj_sample")
    dt_raw = _matmul(x, p["w_dt"], nb, LANES, "dt_proj_sample")
    xbc_c = _ssd_conv_step(proj, jnp.swapaxes(state_ssd_conv, 0, 1), p["ssd_conv_w"], p["ssd_conv_b"], nb)
    rows = SSD_HEADS * SSD_HEADDIM
    ssm, y_t = _ssd_step(xbc_c, dt_raw, p["dt_bias"], p["a_log"], state_ssd.reshape(nb, rows, SSD_STATE), nb)
    ynorm = _gated_norm_step(y_t, xbc_c, proj, p["d_row"], p["norm_w"], nb)
    cact, glu = _conf_step(proj, jnp.swapaxes(state_conf_conv, 0, 1), p["conf_w"], p["conf_b"],
                           p["conf_ln_g"], p["conf_ln_b"], nb)
    attn = _attn_step(proj, cache_mem_k, cache_mem_v, nb)
    y = _tokenwise_tail(p, x, proj, ynorm, cact, attn, nb, nb, nb)
    ssd_buf = jnp.concatenate([state_ssd_conv[:, 1:], proj[:, None, S_Z:S_XBC]], axis=1)
    conf_buf = jnp.concatenate([state_conf_conv[:, 1:], glu[:, None, :]], axis=1)
    return (y.reshape(nb, 1, D_MODEL), ssm.reshape(nb, SSD_HEADS, SSD_HEADDIM, SSD_STATE), ssd_buf, conf_buf)


def kernel(x_prompt, x_sample, state_ssd, state_ssd_conv, state_conf_conv, cache_mem_k, cache_mem_v, mem_prompt, w_in, ssd_conv_w, ssd_conv_b, ssd_dt_bias, ssd_a_log, ssd_d, ssd_norm_w, ssd_w_out, conf_conv_w, conf_conv_b, conf_ln_g, conf_ln_b, conf_w_out, mem_w_k, mem_w_v, mem_w_o, w_out, ln1_g, ln1_b, peer_w_q, peer_sub_keys, peer_u, peer_v, ln2_g, ln2_b):
    assert w_in.shape[0] == DEPTH == 1
    p = _prepare_weights(w_in, ssd_conv_w, ssd_conv_b, ssd_dt_bias, ssd_a_log, ssd_d, ssd_norm_w, ssd_w_out,
                         conf_conv_w, conf_conv_b, conf_ln_g, conf_ln_b, conf_w_out, mem_w_k, mem_w_v, mem_w_o,
                         w_out, ln1_g, ln1_b, peer_w_q, peer_sub_keys, peer_u, peer_v, ln2_g, ln2_b)
    yp, ssm_p, sbuf_p, cbuf_p, mk_p, mv_p = _prompt_layer(p, x_prompt, mem_prompt)
    ys, ssm_s, sbuf_s, cbuf_s = _sample_layer(p, x_sample, state_ssd[0], state_ssd_conv[0], state_conf_conv[0],
                                              cache_mem_k[0], cache_mem_v[0])
    return (yp, ys, ssm_p[None], sbuf_p[None], cbuf_p[None], mk_p[None], mv_p[None],
            ssm_s[None], sbuf_s[None], cbuf_s[None])
```

```python
import functools
import math

import jax
import jax.numpy as jnp
from jax import lax
from jax.experimental import pallas as pl
from jax.experimental.pallas import tpu as pltpu

F32 = jnp.float32
BF16 = jnp.bfloat16

D_MODEL = 1024
SSD_INNER = 2048
SSD_HEADDIM = 64
SSD_HEADS = 32
SSD_GROUPS = 4
SSD_STATE = 128
SSD_CONV = 4
SSD_CHUNK = 128
SSD_XBC = 3072
CONF_DIM = 1024
CONF_WIDTH = 31
MEM_TOKENS = 256
MEM_HEADS = 4
MEM_HEADDIM = 256
PEER_HEADS = 8
PEER_NKEYS = 128
PEER_EXPERTS = PEER_NKEYS * PEER_NKEYS
PEER_DHALF = 128
PEER_TOPK = 16
DEPTH = 1
ALPHA = (2.0 * DEPTH) ** 0.25
LN_EPS = 1e-5
S_Z = SSD_INNER
S_XBC = S_Z + SSD_XBC
S_DT = S_XBC + SSD_HEADS

COL_XBC = 2
COL_CONF_A = 5
COL_CONF_B = 6
COL_QMEM = 7
COL_GATES = 8
PROJ_COLS = 11 * 1024

LANES = 128
SUBLANES = 8
VMEM_LIMIT_BYTES = 56 * 1024 * 1024


def _params(*sem):
    return pltpu.CompilerParams(dimension_semantics=sem, vmem_limit_bytes=VMEM_LIMIT_BYTES)


def _sigmoid(x):
    return jax.nn.sigmoid(x)


def _silu(x):
    return x * _sigmoid(x)


def _twice_gelu_erf(x):
    return x * (1.0 + lax.erf(x * (1.0 / math.sqrt(2.0))))


def _layer_norm_rows(x, g, b):
    mu = jnp.mean(x, axis=-1, keepdims=True)
    xc = x - mu
    var = jnp.mean(xc * xc, axis=-1, keepdims=True)
    return xc * lax.rsqrt(var + LN_EPS) * g + b


def _split3_bf16(x):
    hi = x.astype(BF16)
    r1 = x - hi.astype(F32)
    mid = r1.astype(BF16)
    lo = (r1 - mid.astype(F32)).astype(BF16)
    return hi, mid, lo


def _dot(a, b):
    return jnp.dot(a, b, preferred_element_type=F32)


def _dot_tn(a, b):
    return lax.dot_general(a, b, (((0,), (0,)), ((), ())), preferred_element_type=F32)


def _dot_nt(a, b):
    return lax.dot_general(a, b, (((1,), (1,)), ((), ())), preferred_element_type=F32)


def _mm_kernel(x_ref, w_ref, o_ref):
    o_ref[...] = _dot(x_ref[...].astype(BF16), w_ref[...]).astype(o_ref.dtype)


def _matmul(x, w, tm, tn, name):
    m, k = x.shape
    n = w.shape[1]
    tm = min(tm, m)
    assert m % tm == 0 and n % tn == 0
    return pl.pallas_call(
        _mm_kernel,
        grid=(m // tm, n // tn),
        in_specs=[pl.BlockSpec((tm, k), lambda i, j: (i, 0)),
                  pl.BlockSpec((k, tn), lambda i, j: (0, j))],
        out_specs=pl.BlockSpec((tm, tn), lambda i, j: (i, j)),
        out_shape=jax.ShapeDtypeStruct((m, n), F32),
        compiler_params=_params("parallel", "parallel"),
        name=name,
    )(x, w)


SSD_CONV_ROWS = 512
CONV_CHUNK = 32


def _causal_conv_rows(src, r0, cols, w_ref, bias_ref, taps, first_off):
    win = -(-(first_off + taps - 1 + CONV_CHUNK) // SUBLANES) * SUBLANES
    window = src[r0:r0 + win, cols]
    acc = jnp.broadcast_to(bias_ref[:, cols], (CONV_CHUNK, LANES))
    for rho in range(SUBLANES):
        ks = [k for k in range(taps) if (first_off + k) % SUBLANES == rho]
        if not ks:
            continue
        shifted = window if rho == 0 else pltpu.roll(window, win - rho, 0)
        for k in ks:
            a = (first_off + k) // SUBLANES * SUBLANES
            acc = acc + w_ref[k:k + 1, cols] * shifted[a:a + CONV_CHUNK, :]
    return acc


def _ssd_conv_kernel(x_ref, halo_ref, w_ref, b_ref, o_ref, scr):
    tl = x_ref.shape[0]
    first = pl.program_id(1) == 0
    scr[0:SUBLANES, :] = jnp.where(first, 0.0, halo_ref[...])
    scr[SUBLANES:SUBLANES + tl, :] = x_ref[...]
    for r0 in range(0, tl, CONV_CHUNK):
        for c0 in range(0, x_ref.shape[1], LANES):
            cols = slice(c0, c0 + LANES)
            acc = _causal_conv_rows(scr, r0, cols, w_ref, b_ref, SSD_CONV, SUBLANES - (SSD_CONV - 1))
            o_ref[r0:r0 + CONV_CHUNK, cols] = _silu(acc)


def _ssd_conv_prompt(proj, w, b, nb, seq):
    tl = SSD_CONV_ROWS
    nl = seq // tl
    rows_per_halo = tl // SUBLANES
    return pl.pallas_call(
        _ssd_conv_kernel,
        grid=(nb, nl, SSD_XBC // 1024),
        in_specs=[
            pl.BlockSpec((tl, 1024), lambda bi, l, j: (bi * nl + l, COL_XBC + j)),
            pl.BlockSpec((SUBLANES, 1024),
                         lambda bi, l, j: (jnp.maximum((bi * nl + l) * rows_per_halo - 1, 0), COL_XBC + j)),
            pl.BlockSpec((SSD_CONV, 1024), lambda bi, l, j: (0, j)),
            pl.BlockSpec((1, 1024), lambda bi, l, j: (0, j)),
        ],
        out_specs=pl.BlockSpec((tl, 1024), lambda bi, l, j: (bi * nl + l, j)),
        out_shape=jax.ShapeDtypeStruct((nb * seq, SSD_XBC), F32),
        scratch_shapes=[pltpu.VMEM((SUBLANES + tl, 1024), F32)],
        compiler_params=_params("parallel", "parallel", "parallel"),
        name="ssd_conv_prompt",
    )(proj, proj, w, b)


def _softplus(x):
    return jnp.maximum(x, 0.0) + jnp.log1p(jnp.exp(-jnp.abs(x)))


def _ssd_scan_kernel(xc_ref, z_ref, dtraw_ref, dtb_ref, alog_ref, d_ref, nw_ref,
                     y_ref, st_ref, yscr):
    q = SSD_CHUNK
    c = pl.program_id(1)

    @pl.when(c == 0)
    def _():
        st_ref[...] = jnp.zeros_like(st_ref)

    dt = _softplus(dtraw_ref[...] + dtb_ref[...])
    a = -jnp.exp(alog_ref[...])
    da = dt * a
    row = lax.broadcasted_iota(jnp.int32, (q, q), 0)
    col = lax.broadcasted_iota(jnp.int32, (q, q), 1)
    causal = row >= col
    tri = jnp.where(causal, 1.0, 0.0).astype(BF16)
    hi, mid, lo = _split3_bf16(da)
    a_cs = _dot(tri, hi) + _dot(tri, mid) + _dot(tri, lo)
    a_cs_t = a_cs.T
    dt_t = dt.T
    xs = xc_ref[:, 0:SSD_INNER]
    xs_t = xs.T
    heads_per_group = SSD_HEADS // SSD_GROUPS
    rows_per_group = heads_per_group * SSD_HEADDIM
    first_half = lax.broadcasted_iota(jnp.int32, (q, 2 * SSD_HEADDIM), 1) < SSD_HEADDIM
    per_head_rows = lambda vals: jnp.concatenate(
        [jnp.broadcast_to(v, (SSD_HEADDIM, v.shape[1])) for v in vals], axis=0)
    for g in range(SSD_GROUPS):
        heads = range(g * heads_per_group, (g + 1) * heads_per_group)
        rows = slice(g * rows_per_group, (g + 1) * rows_per_group)
        b_g16 = xc_ref[:, SSD_INNER + g * SSD_STATE:SSD_INNER + (g + 1) * SSD_STATE].astype(BF16)
        c_g16 = xc_ref[:, SSD_INNER + (SSD_GROUPS + g) * SSD_STATE:
                       SSD_INNER + (SSD_GROUPS + g + 1) * SSD_STATE].astype(BF16)
        cb = _dot_nt(c_g16, b_g16)
        st_g = st_ref[0, rows, :]
        y_off = _dot_nt(c_g16, st_g.astype(BF16))
        last = [a_cs_t[h:h + 1, q - 1:q] for h in heads]
        w_rows = per_head_rows([jnp.exp(last[i] - a_cs_t[h:h + 1, :]) * dt_t[h:h + 1, :]
                                for i, h in enumerate(heads)])
        states = _dot((xs_t[rows, :] * w_rows).astype(BF16), b_g16)
        st_ref[0, rows, :] = st_g * per_head_rows([jnp.exp(v) * jnp.ones((1, SSD_STATE), F32) for v in last]) + states
        for r in range(0, heads_per_group, 2):
            h0 = g * heads_per_group + r
            m1, ea = [], []
            for h in (h0, h0 + 1):
                acs_col = a_cs[:, h:h + 1]
                seg = jnp.where(causal, acs_col - a_cs_t[h:h + 1, :], -1e30)
                m1.append((cb * jnp.exp(seg) * dt_t[h:h + 1, :]).astype(BF16))
                ea.append(jnp.broadcast_to(jnp.exp(acs_col), (q, SSD_HEADDIM)))
            cols = slice(h0 * SSD_HEADDIM, (h0 + 2) * SSD_HEADDIM)
            x_pair = xs[:, cols]
            x_diag = jnp.concatenate([jnp.where(first_half, x_pair, 0.0), jnp.where(first_half, 0.0, x_pair)],
                                     axis=0).astype(BF16)
            off_cols = slice(r * SSD_HEADDIM, (r + 2) * SSD_HEADDIM)
            yscr[:, cols] = (_dot(jnp.concatenate(m1, axis=1), x_diag)
                             + y_off[:, off_cols] * jnp.concatenate(ea, axis=1))
    y = yscr[...] + xs * d_ref[...]
    hgate = y * _silu(z_ref[...])
    ms = jnp.mean(hgate * hgate, axis=-1, keepdims=True)
    y_ref[...] = (hgate * lax.rsqrt(ms + LN_EPS) * nw_ref[...]).astype(y_ref.dtype)


def _ssd_scan_prompt(xbc_c, proj, dt_raw, dt_bias, a_log, d_row, norm_w, nb, seq):
    q = SSD_CHUNK
    nc = seq // q
    return pl.pallas_call(
        _ssd_scan_kernel,
        grid=(nb, nc),
        in_specs=[
            pl.BlockSpec((q, SSD_XBC), lambda bi, c: (bi * nc + c, 0)),
            pl.BlockSpec((q, SSD_INNER), lambda bi, c: (bi * nc + c, 0)),
            pl.BlockSpec((q, LANES), lambda bi, c: (bi * nc + c, 0)),
            pl.BlockSpec((1, LANES), lambda bi, c: (0, 0)),
            pl.BlockSpec((1, LANES), lambda bi, c: (0, 0)),
            pl.BlockSpec((1, SSD_INNER), lambda bi, c: (0, 0)),
            pl.BlockSpec((1, SSD_INNER), lambda bi, c: (0, 0)),
        ],
        out_specs=[
            pl.BlockSpec((q, SSD_INNER), lambda bi, c: (bi * nc + c, 0)),
            pl.BlockSpec((1, SSD_HEADS * SSD_HEADDIM, SSD_STATE), lambda bi, c: (bi, 0, 0)),
        ],
        out_shape=[jax.ShapeDtypeStruct((nb * seq, SSD_INNER), BF16),
                   jax.ShapeDtypeStruct((nb, SSD_HEADS * SSD_HEADDIM, SSD_STATE), F32)],
        scratch_shapes=[pltpu.VMEM((q, SSD_INNER), F32)],
        compiler_params=_params("parallel", "arbitrary"),
        name="ssd_scan_prompt",
    )(xbc_c, proj, dt_raw, dt_bias, a_log, d_row, norm_w)


def _ssd_conv_step_kernel(x_ref, st_ref, w_ref, b_ref, o_ref):
    acc = b_ref[...] + w_ref[SSD_CONV - 1:SSD_CONV, :] * x_ref[...]
    for k in range(SSD_CONV - 1):
        acc = acc + w_ref[k:k + 1, :] * st_ref[k]
    o_ref[...] = _silu(acc)


def _ssd_conv_step(proj, state_t, w, b, nb):
    return pl.pallas_call(
        _ssd_conv_step_kernel,
        grid=(SSD_XBC // 1024,),
        in_specs=[
            pl.BlockSpec((nb, 1024), lambda j: (0, COL_XBC + j)),
            pl.BlockSpec((SSD_CONV - 1, nb, 1024), lambda j: (0, 0, j)),
            pl.BlockSpec((SSD_CONV, 1024), lambda j: (0, j)),
            pl.BlockSpec((1, 1024), lambda j: (0, j)),
        ],
        out_specs=pl.BlockSpec((nb, 1024), lambda j: (0, j)),
        out_shape=jax.ShapeDtypeStruct((nb, SSD_XBC), F32),
        compiler_params=_params("parallel"),
        name="ssd_conv_step",
    )(proj, state_t, w, b)


def _ssd_step_kernel(xc_ref, bc_ref, dtraw_ref, dtb_ref, alog_ref, st_ref, st_out_ref, yt_ref, xdt_scr, dec_scr):
    nb = xc_ref.shape[0]
    bi = pl.program_id(0)
    rows = SSD_HEADS * SSD_HEADDIM

    @pl.when(bi == 0)
    def _():
        dt = _softplus(dtraw_ref[...] + dtb_ref[...])
        a = -jnp.exp(alog_ref[...])
        dt_t = dt.T
        da_t = (dt * a).T
        expand = lambda t: jnp.concatenate(
            [jnp.broadcast_to(t[h:h + 1, :], (SSD_HEADDIM, nb)) for h in range(SSD_HEADS)], axis=0)
        xs_t = xc_ref[:, 0:SSD_INNER].T
        xdt_scr[...] = xs_t * expand(dt_t)
        dec_scr[...] = jnp.exp(expand(da_t))
        yt_ref[...] = jnp.zeros_like(yt_ref)

    lane_b = lax.broadcasted_iota(jnp.int32, (nb, LANES), 0)
    onehot = jnp.where(lane_b == bi, 1.0, 0.0).astype(BF16)

    def pick(ref):
        hi, mid, lo = _split3_bf16(ref[...])
        return _dot(hi, onehot) + _dot(mid, onehot) + _dot(lo, onehot)

    xdt_b = pick(xdt_scr)
    dec_b = pick(dec_scr)
    out_lane = lax.broadcasted_iota(jnp.int32, (rows // SSD_GROUPS, LANES), 1) == bi
    for g in range(SSD_GROUPS):
        r0 = g * (rows // SSD_GROUPS)
        r1 = r0 + rows // SSD_GROUPS
        b_row = bc_ref[0, :, g * SSD_STATE:(g + 1) * SSD_STATE]
        c_row = bc_ref[0, :, (SSD_GROUPS + g) * SSD_STATE:(SSD_GROUPS + g + 1) * SSD_STATE]
        h_new = st_ref[0, r0:r1, :] * dec_b[r0:r1, :] + xdt_b[r0:r1, :] * b_row
        st_out_ref[0, r0:r1, :] = h_new
        y_col = jnp.sum(h_new * c_row, axis=-1, keepdims=True)
        yt_ref[r0:r1, :] = jnp.where(out_lane, y_col, yt_ref[r0:r1, :])


def _ssd_step(xbc_c, dt_raw, dt_bias, a_log, state, nb):
    rows = SSD_HEADS * SSD_HEADDIM
    assert nb == LANES
    bc_rows = xbc_c[:, SSD_INNER:].reshape(nb, 1, SSD_XBC - SSD_INNER)
    return pl.pallas_call(
        _ssd_step_kernel,
        grid=(nb,),
        in_specs=[
            pl.BlockSpec((nb, SSD_XBC), lambda bi: (0, 0)),
            pl.BlockSpec((1, 1, bc_rows.shape[2]), lambda bi: (bi, 0, 0)),
            pl.BlockSpec((nb, LANES), lambda bi: (0, 0)),
            pl.BlockSpec((1, LANES), lambda bi: (0, 0)),
            pl.BlockSpec((1, LANES), lambda bi: (0, 0)),
            pl.BlockSpec((1, rows, SSD_STATE), lambda bi: (bi, 0, 0)),
        ],
        out_specs=[
            pl.BlockSpec((1, rows, SSD_STATE), lambda bi: (bi, 0, 0)),
            pl.BlockSpec((rows, nb), lambda bi: (0, 0)),
        ],
        out_shape=[jax.ShapeDtypeStruct((nb, rows, SSD_STATE), F32),
                   jax.ShapeDtypeStruct((rows, nb), F32)],
        scratch_shapes=[pltpu.VMEM((rows, nb), F32), pltpu.VMEM((rows, nb), F32)],
        compiler_params=_params("arbitrary"),
        name="ssd_step",
    )(xbc_c, bc_rows, dt_raw, dt_bias, a_log, state)


def _gated_norm_step_kernel(yt_ref, xc_ref, z_ref, d_ref, nw_ref, o_ref):
    y = yt_ref[...].T + xc_ref[:, 0:SSD_INNER] * d_ref[...]
    hgate = y * _silu(z_ref[...])
    ms = jnp.mean(hgate * hgate, axis=-1, keepdims=True)
    o_ref[...] = (hgate * lax.rsqrt(ms + LN_EPS) * nw_ref[...]).astype(o_ref.dtype)


def _gated_norm_step(y_t, xbc_c, proj, d_row, norm_w, nb):
    return pl.pallas_call(
        _gated_norm_step_kernel,
        grid=(1,),
        in_specs=[
            pl.BlockSpec((SSD_INNER, nb), lambda i: (0, 0)),
            pl.BlockSpec((nb, SSD_XBC), lambda i: (0, 0)),
            pl.BlockSpec((nb, SSD_INNER), lambda i: (0, 0)),
            pl.BlockSpec((1, SSD_INNER), lambda i: (0, 0)),
            pl.BlockSpec((1, SSD_INNER), lambda i: (0, 0)),
        ],
        out_specs=pl.BlockSpec((nb, SSD_INNER), lambda i: (0, 0)),
        out_shape=jax.ShapeDtypeStruct((nb, SSD_INNER), BF16),
        compiler_params=_params("arbitrary"),
        name="ssd_gated_norm_step",
    )(y_t, xbc_c, proj, d_row, norm_w)


CONF_ROWS = 256
CONF_PAD = 32
CONF_W_ROWS = 32


def _conf_kernel(a_ref, b_ref, w_ref, cb_ref, g_ref, beta_ref, o_ref, st_ref, gl, cv):
    tl = a_ref.shape[0]
    l = pl.program_id(1)

    @pl.when(l == 0)
    def _():
        gl[0:CONF_PAD, :] = jnp.zeros((CONF_PAD, CONF_DIM), F32)

    gl[CONF_PAD:CONF_PAD + tl, :] = a_ref[...] * _sigmoid(b_ref[...])
    for r0 in range(0, tl, CONV_CHUNK):
        for c0 in range(0, CONF_DIM, LANES):
            cols = slice(c0, c0 + LANES)
            cv[r0:r0 + CONV_CHUNK, cols] = _causal_conv_rows(gl, r0, cols, w_ref, cb_ref, CONF_WIDTH,
                                                             CONF_PAD - (CONF_WIDTH - 1))
    o_ref[...] = _silu(_layer_norm_rows(cv[...], g_ref[...], beta_ref[...])).astype(o_ref.dtype)

    @pl.when(l == pl.num_programs(1) - 1)
    def _():
        st_ref[0] = gl[CONF_PAD + tl - (CONF_WIDTH - 1):CONF_PAD + tl, :]

    gl[0:CONF_PAD, :] = gl[tl:tl + CONF_PAD, :]


def _conf_prompt(proj, w, cb, ln_g, ln_b, nb, seq):
    tl = CONF_ROWS
    nl = seq // tl
    vec = pl.BlockSpec((1, CONF_DIM), lambda bi, l: (0, 0))
    return pl.pallas_call(
        _conf_kernel,
        grid=(nb, nl),
        in_specs=[
            pl.BlockSpec((tl, 1024), lambda bi, l: (bi * nl + l, COL_CONF_A)),
            pl.BlockSpec((tl, 1024), lambda bi, l: (bi * nl + l, COL_CONF_B)),
            pl.BlockSpec((CONF_W_ROWS, CONF_DIM), lambda bi, l: (0, 0)),
            vec, vec, vec,
        ],
        out_specs=[
            pl.BlockSpec((tl, CONF_DIM), lambda bi, l: (bi * nl + l, 0)),
            pl.BlockSpec((1, CONF_WIDTH - 1, CONF_DIM), lambda bi, l: (bi, 0, 0)),
        ],
        out_shape=[jax.ShapeDtypeStruct((nb * seq, CONF_DIM), BF16),
                   jax.ShapeDtypeStruct((nb, CONF_WIDTH - 1, CONF_DIM), F32)],
        scratch_shapes=[pltpu.VMEM((CONF_PAD + tl, CONF_DIM), F32), pltpu.VMEM((tl, CONF_DIM), F32)],
        compiler_params=_params("parallel", "arbitrary"),
        name="conf_prompt",
    )(proj, proj, w, cb, ln_g, ln_b)


def _conf_step_kernel(a_ref, b_ref, st_ref, w_ref, cb_ref, g_ref, beta_ref, o_ref, glu_ref):
    glu = a_ref[...] * _sigmoid(b_ref[...])
    glu_ref[...] = glu
    acc = cb_ref[...] + w_ref[CONF_WIDTH - 1:CONF_WIDTH, :] * glu
    for k in range(CONF_WIDTH - 1):
        acc = acc + w_ref[k:k + 1, :] * st_ref[k]
    o_ref[...] = _silu(_layer_norm_rows(acc, g_ref[...], beta_ref[...])).astype(o_ref.dtype)


def _conf_step(proj, state_t, w, cb, ln_g, ln_b, nb):
    vec = pl.BlockSpec((1, CONF_DIM), lambda i: (0, 0))
    return pl.pallas_call(
        _conf_step_kernel,
        grid=(1,),
        in_specs=[
            pl.BlockSpec((nb, 1024), lambda i: (0, COL_CONF_A)),
            pl.BlockSpec((nb, 1024), lambda i: (0, COL_CONF_B)),
            pl.BlockSpec((CONF_WIDTH - 1, nb, CONF_DIM), lambda i: (0, 0, 0)),
            pl.BlockSpec((CONF_W_ROWS, CONF_DIM), lambda i: (0, 0)),
            vec, vec, vec,
        ],
        out_specs=[pl.BlockSpec((nb, CONF_DIM), lambda i: (0, 0)),
                   pl.BlockSpec((nb, CONF_DIM), lambda i: (0, 0))],
        out_shape=[jax.ShapeDtypeStruct((nb, CONF_DIM), BF16),
                   jax.ShapeDtypeStruct((nb, CONF_DIM), F32)],
        compiler_params=_params("arbitrary"),
        name="conf_step",
    )(proj, proj, state_t, w, cb, ln_g, ln_b)


ATTN_ROWS = 512
MEM_SCALE = MEM_HEADDIM ** -0.5


def _softmax_lanes(sc):
    mx = jnp.max(sc, axis=-1, keepdims=True)
    e = jnp.exp(sc - mx)
    return e / jnp.sum(e, axis=-1, keepdims=True)


def _attn_kernel(q_ref, k_ref, v_ref, o_ref):
    for h in range(MEM_HEADS):
        sl = slice(h * MEM_HEADDIM, (h + 1) * MEM_HEADDIM)
        sc = _dot_nt(q_ref[:, sl].astype(BF16), k_ref[:, sl].astype(BF16)) * MEM_SCALE
        pr = _softmax_lanes(sc).astype(BF16)
        o_ref[:, sl] = _dot(pr, v_ref[:, sl].astype(BF16)).astype(o_ref.dtype)


def _attn_prompt(proj, mk, mv, nb, seq):
    tq = ATTN_ROWS
    nq = seq // tq
    kv = pl.BlockSpec((MEM_TOKENS, 1024), lambda bi, i: (bi, 0))
    return pl.pallas_call(
        _attn_kernel,
        grid=(nb, nq),
        in_specs=[pl.BlockSpec((tq, 1024), lambda bi, i: (bi * nq + i, COL_QMEM)), kv, kv],
        out_specs=pl.BlockSpec((tq, 1024), lambda bi, i: (bi * nq + i, 0)),
        out_shape=jax.ShapeDtypeStruct((nb * seq, 1024), BF16),
        compiler_params=_params("parallel", "parallel"),
        name="attn_prompt",
    )(proj, mk, mv)


ATTN_STEP_BATCH = 2


def _attn_step_kernel(q_ref, k_ref, v_ref, o_ref):
    for i in range(ATTN_STEP_BATCH):
        q = q_ref[i]
        sc = jnp.sum(k_ref[i] * q[None], axis=-1, keepdims=True) * MEM_SCALE
        e = jnp.exp(sc - jnp.max(sc, axis=0, keepdims=True))
        pr = e / jnp.sum(e, axis=0, keepdims=True)
        o_ref[i] = jnp.sum(pr * v_ref[i], axis=0)


def _attn_step(proj, mem_k, mem_v, nb):
    bb = ATTN_STEP_BATCH
    q = proj[:, COL_QMEM * 1024:(COL_QMEM + 1) * 1024].reshape(nb, MEM_HEADS, MEM_HEADDIM)
    kv = pl.BlockSpec((bb, MEM_TOKENS, MEM_HEADS, MEM_HEADDIM), lambda i: (i, 0, 0, 0))
    qo = pl.BlockSpec((bb, MEM_HEADS, MEM_HEADDIM), lambda i: (i, 0, 0))
    out = pl.pallas_call(
        _attn_step_kernel,
        grid=(nb // bb,),
        in_specs=[qo, kv, kv],
        out_specs=qo,
        out_shape=jax.ShapeDtypeStruct((nb, MEM_HEADS, MEM_HEADDIM), F32),
        compiler_params=_params("parallel"),
        name="attn_step",
    )(q, mem_k, mem_v)
    return out.reshape(nb, MEM_HEADS * MEM_HEADDIM)


def _merge_kernel(yn_ref, ca_ref, at_ref, ga_ref, gb_ref, gc_ref, x_ref, wa_ref, wb_ref, wc_ref, wo_ref,
                  g_ref, b_ref, o_ref):
    br_a = _dot(yn_ref[...].astype(BF16), wa_ref[...])
    br_b = _dot(ca_ref[...].astype(BF16), wb_ref[...])
    br_c = _dot(at_ref[...].astype(BF16), wc_ref[...])
    merged = _sigmoid(ga_ref[...]) * br_a + _sigmoid(gb_ref[...]) * br_b + _sigmoid(gc_ref[...]) * br_c
    res = ALPHA * x_ref[...] + _dot(merged.astype(BF16), wo_ref[...])
    o_ref[...] = _layer_norm_rows(res, g_ref[...], b_ref[...])


def _merge(ynorm, cact, attn, proj, x, w_a, w_b, w_c, w_o, ln_g, ln_b, tm):
    t = x.shape[0]
    row = lambda w: pl.BlockSpec((tm, w), lambda i: (i, 0))
    gate = lambda j: pl.BlockSpec((tm, D_MODEL), lambda i: (i, COL_GATES + j))
    full = lambda a: pl.BlockSpec(a.shape, lambda i: (0, 0))
    return pl.pallas_call(
        _merge_kernel,
        grid=(t // tm,),
        in_specs=[row(SSD_INNER), row(CONF_DIM), row(1024), gate(0), gate(1), gate(2),
                  row(D_MODEL), full(w_a), full(w_b), full(w_c), full(w_o), full(ln_g), full(ln_b)],
        out_specs=row(D_MODEL),
        out_shape=jax.ShapeDtypeStruct((t, D_MODEL), F32),
        compiler_params=_params("parallel"),
        name="merge_ln1",
    )(ynorm, cact, attn, proj, proj, proj, x, w_a, w_b, w_c, w_o, ln_g, ln_b)


N_SUBKEYS = 2 * PEER_HEADS
STAIRCASE = [(i, j) for i in range(PEER_TOPK) for j in range(PEER_TOPK) if (i + 1) * (j + 1) <= PEER_TOPK]


def _compare_exchange(v, i, j):
    a, b = v[i], v[j]
    v[i] = jnp.maximum(a, b)
    v[j] = jnp.minimum(a, b)


def _bitonic_merge_desc(v):
    n = len(v)
    j = n // 2
    while j >= 1:
        for i in range(n):
            if i ^ j > i:
                _compare_exchange(v, i, i ^ j)
        j //= 2


def _bitonic_sort_desc(v):
    n = len(v)
    k = 2
    while k <= n:
        j = k // 2
        while j >= 1:
            for i in range(n):
                l = i ^ j
                if l > i:
                    if i & k == 0:
                        _compare_exchange(v, i, l)
                    else:
                        _compare_exchange(v, l, i)
            j //= 2
        k *= 2


def _top16_sorted(s_t):
    v = [s_t[r * SUBLANES:(r + 1) * SUBLANES, :] for r in range(PEER_NKEYS // SUBLANES)]
    _bitonic_sort_desc(v)
    for shift in (4, 2, 1):
        other = [pltpu.roll(x, shift, 0) for x in v]
        v = [jnp.maximum(v[i], other[PEER_TOPK - 1 - i]) for i in range(PEER_TOPK)]
        _bitonic_merge_desc(v)
    return v


def _prefix_length(test, vals):
    w = jnp.where
    c8 = test(vals[7])
    c4 = test(w(c8, vals[11], vals[3]))
    c2 = test(w(c8, w(c4, vals[13], vals[9]), w(c4, vals[5], vals[1])))
    c1 = test(w(c8, w(c4, w(c2, vals[14], vals[12]), w(c2, vals[10], vals[8])),
                w(c4, w(c2, vals[6], vals[4]), w(c2, vals[2], vals[0]))))
    c16 = test(vals[15])
    return w(c8, 8.0, 0.0) + w(c4, 4.0, 0.0) + w(c2, 2.0, 0.0) + w(c1, 1.0, 0.0) + w(c16, 1.0, 0.0)


def _pack_factor():
    return 4 // jnp.dtype(BF16).itemsize


def _pack_rows(x):
    return pltpu.bitcast(x.astype(BF16), jnp.uint32)


def _unpack_rows(w):
    return pltpu.bitcast(w, BF16)


def _peer_score_kernel(x_ref, wq_ref, keys_ref, npass_ref, e0_ref, rank_ref, e1_ref, qv_scr, top_scr, s1_scr):
    tm = x_ref.shape[0]
    qv_scr[...] = _dot(x_ref[...].astype(BF16), wq_ref[...]).astype(BF16)
    for h in range(PEER_HEADS):
        q_h = qv_scr[:, 2 * h * PEER_DHALF:2 * (h + 1) * PEER_DHALF]
        s_both = _dot_nt(keys_ref[h], q_h)
        for k in range(2):
            s_t = s_both[k * PEER_NKEYS:(k + 1) * PEER_NKEYS, :]
            if k == 0:
                npass_ref[h] = s_t
            else:
                s1_scr[h] = s_t
            top = _top16_sorted(s_t)
            for i in range(PEER_TOPK):
                top_scr[k, i, h:h + 1, :] = top[i][0:1, :]
    a = [top_scr[0, i] for i in range(PEER_TOPK)]
    b = [top_scr[1, i] for i in range(PEER_TOPK)]
    cand = [a[i] + b[j] for i, j in STAIRCASE]
    ordered = cand + [jnp.full((PEER_HEADS, tm), -jnp.inf, F32)] * (64 - len(cand))
    _bitonic_sort_desc(ordered)
    tau = ordered[PEER_TOPK - 1]
    top_sum = a[0] + b[0]
    z = jnp.zeros((PEER_HEADS, tm), F32)
    for c in cand:
        z = z + jnp.where(c >= tau, jnp.exp(c - top_sum), 0.0)
    half_inv_z = 0.5 / z
    for h in range(PEER_HEADS):
        s0 = npass_ref[h]
        s1 = s1_scr[h]
        e0_ref[h] = jnp.exp(s0 - a[0][h:h + 1, :])
        e1_ref[h] = _pack_rows(jnp.exp(s1 - b[0][h:h + 1, :]) * half_inv_z[h:h + 1, :])
        b_rows = [b[r][h:h + 1, :] for r in range(PEER_TOPK)]
        tau_h = tau[h:h + 1, :]
        npass_ref[h] = _prefix_length(lambda v: s0 + v >= tau_h, b_rows)
        rank_ref[h] = _pack_rows(_prefix_length(lambda v: v > s1, b_rows))


def _peer_scores(x, w_q, keys, tm):
    t = x.shape[0]
    spec = pl.BlockSpec((PEER_HEADS, PEER_NKEYS, tm), lambda i: (0, 0, i))
    spec16 = pl.BlockSpec((PEER_HEADS, PEER_NKEYS // _pack_factor(), tm), lambda i: (0, 0, i))
    shape = jax.ShapeDtypeStruct((PEER_HEADS, PEER_NKEYS, t), F32)
    shape16 = jax.ShapeDtypeStruct((PEER_HEADS, PEER_NKEYS // _pack_factor(), t), jnp.uint32)
    return pl.pallas_call(
        _peer_score_kernel,
        grid=(t // tm,),
        in_specs=[pl.BlockSpec((tm, D_MODEL), lambda i: (i, 0)),
                  pl.BlockSpec(w_q.shape, lambda i: (0, 0)),
                  pl.BlockSpec(keys.shape, lambda i: (0, 0, 0))],
        out_specs=[spec, spec, spec16, spec16],
        out_shape=[shape, shape, shape16, shape16],
        scratch_shapes=[pltpu.VMEM((tm, N_SUBKEYS * PEER_DHALF), BF16),
                        pltpu.VMEM((2, PEER_TOPK, PEER_HEADS, tm), F32),
                        pltpu.VMEM((PEER_HEADS, PEER_NKEYS, tm), F32)],
        compiler_params=_params("parallel"),
        name="peer_scores",
    )(x, w_q, keys)


PEER_UNIT = 1024
I_PER_UNIT = PEER_UNIT // PEER_NKEYS
assert I_PER_UNIT == SUBLANES
UNITS = PEER_EXPERTS // PEER_UNIT
UNITS_PER_STEP = 2
II_GROUP = 1


def _peer_gate_chunk(act_ref, h_ref, npass_ref, e0_ref, rank_ref, e1_ref, c0, ii0):
    gate_sub = SUBLANES * _pack_factor()
    cols = slice(c0, c0 + LANES)
    sub = (gate_sub, LANES)
    zero = jnp.zeros(sub, BF16)
    row = lambda ref, h, ii: jnp.broadcast_to(ref[h, ii:ii + 1, cols], sub).astype(BF16)
    npass = [[row(npass_ref, h, ii0 + q) for h in range(PEER_HEADS)] for q in range(II_GROUP)]
    e0 = [[row(e0_ref, h, ii0 + q) for h in range(PEER_HEADS)] for q in range(II_GROUP)]
    for j0 in range(0, PEER_NKEYS, gate_sub):
        words = slice(j0 // _pack_factor(), j0 // _pack_factor() + SUBLANES)
        gate = [zero] * II_GROUP
        for h in range(PEER_HEADS):
            rank = _unpack_rows(rank_ref[h, words, cols])
            e1 = _unpack_rows(e1_ref[h, words, cols])
            for q in range(II_GROUP):
                gate[q] = gate[q] + e0[q][h] * jnp.where(rank < npass[q][h], e1, zero)
        for q in range(II_GROUP):
            r0 = (ii0 + q) * PEER_NKEYS + j0
            h_ref[r0:r0 + gate_sub, cols] = gate[q] * _twice_gelu_erf(act_ref[r0:r0 + gate_sub, cols]).astype(BF16)


def _peer_unit(tm, u_ref, v_ref, off, npass_ref, e0_ref, rank_ref, e1_ref, xb, act, h, acc_t):
    act[...] = _dot_nt(u_ref[off:off + PEER_UNIT, :], xb[...])
    for c0 in range(0, tm, LANES):
        for ii in range(0, I_PER_UNIT, II_GROUP):
            _peer_gate_chunk(act, h, npass_ref, e0_ref, rank_ref, e1_ref, c0, ii)
    acc_t[...] += _dot_tn(v_ref[off:off + PEER_UNIT, :], h[...])


def _peer_expert_kernel(x_ref, u_ref, v_ref, np_a, e0_a, np_b, e0_b, rank_ref, e1_ref, g_ref, b_ref, o_ref,
                        acc_t, act0, act1, h0, h1, xb):
    tm = x_ref.shape[0]
    step = pl.program_id(1)

    @pl.when(step == 0)
    def _():
        acc_t[...] = jnp.zeros_like(acc_t)
        xb[...] = x_ref[...].astype(BF16)

    _peer_unit(tm, u_ref, v_ref, 0, np_a, e0_a, rank_ref, e1_ref, xb, act0, h0, acc_t)
    _peer_unit(tm, u_ref, v_ref, PEER_UNIT, np_b, e0_b, rank_ref, e1_ref, xb, act1, h1, acc_t)

    @pl.when(step == pl.num_programs(1) - 1)
    def _():
        res = ALPHA * x_ref[...] + acc_t[...].T
        o_ref[...] = _layer_norm_rows(res, g_ref[...], b_ref[...])


def _peer_experts(x, u16, v16, npass, e0, rank, e1, ln_g, ln_b, tm):
    t = x.shape[0]
    te = PEER_UNIT * UNITS_PER_STEP
    group = lambda q: pl.BlockSpec((PEER_HEADS, SUBLANES, tm), lambda i, s: (0, UNITS_PER_STEP * s + q, i))
    full = pl.BlockSpec((PEER_HEADS, PEER_NKEYS // _pack_factor(), tm), lambda i, s: (0, 0, i))
    vec = pl.BlockSpec((1, D_MODEL), lambda i, s: (0, 0))
    return pl.pallas_call(
        _peer_expert_kernel,
        grid=(t // tm, UNITS // UNITS_PER_STEP),
        in_specs=[pl.BlockSpec((tm, D_MODEL), lambda i, s: (i, 0)),
                  pl.BlockSpec((te, D_MODEL), lambda i, s: (s, 0)),
                  pl.BlockSpec((te, D_MODEL), lambda i, s: (s, 0)),
                  group(0), group(0), group(1), group(1), full, full, vec, vec],
        out_specs=pl.BlockSpec((tm, D_MODEL), lambda i, s: (i, 0)),
        out_shape=jax.ShapeDtypeStruct((t, D_MODEL), F32),
        scratch_shapes=[pltpu.VMEM((D_MODEL, tm), F32),
                        pltpu.VMEM((PEER_UNIT, tm), F32), pltpu.VMEM((PEER_UNIT, tm), F32),
                        pltpu.VMEM((PEER_UNIT, tm), BF16), pltpu.VMEM((PEER_UNIT, tm), BF16),
                        pltpu.VMEM((tm, D_MODEL), BF16)],
        compiler_params=_params("parallel", "arbitrary"),
        name="peer_experts",
    )(x, u16, v16, npass, e0, npass, e0, rank, e1, ln_g, ln_b)


def _block_diag_keys(sub_keys):
    zero = jnp.zeros_like(sub_keys[:, 0])
    return jnp.concatenate([jnp.concatenate([sub_keys[:, 0], zero], axis=2),
                            jnp.concatenate([zero, sub_keys[:, 1]], axis=2)], axis=1)


def _pad_lanes(v):
    return jnp.pad(v, (0, LANES - v.shape[0])).reshape(1, LANES)


def _prepare_weights(w_in, ssd_conv_w, ssd_conv_b, ssd_dt_bias, ssd_a_log, ssd_d, ssd_norm_w, ssd_w_out,
                     conf_conv_w, conf_conv_b, conf_ln_g, conf_ln_b, conf_w_out, mem_w_k, mem_w_v, mem_w_o,
                     w_out, ln1_g, ln1_b, peer_w_q, peer_sub_keys, peer_u, peer_v, ln2_g, ln2_b):
    w = w_in[0]
    row = lambda v: v[0].reshape(1, -1)
    return dict(
        w_main=jnp.concatenate([w[:, :S_XBC], w[:, S_DT:]], axis=1).astype(BF16),
        w_dt=jnp.pad(w[:, S_XBC:S_DT], ((0, 0), (0, LANES - SSD_HEADS))).astype(BF16),
        ssd_conv_w=ssd_conv_w[0], ssd_conv_b=row(ssd_conv_b),
        dt_bias=_pad_lanes(ssd_dt_bias[0]), a_log=_pad_lanes(ssd_a_log[0]),
        d_row=jnp.repeat(ssd_d[0], SSD_HEADDIM).reshape(1, SSD_INNER), norm_w=row(ssd_norm_w),
        ssd_w_out=ssd_w_out[0].astype(BF16),
        conf_w=jnp.pad(conf_conv_w[0], ((0, CONF_W_ROWS - CONF_WIDTH), (0, 0))), conf_b=row(conf_conv_b),
        conf_ln_g=row(conf_ln_g), conf_ln_b=row(conf_ln_b), conf_w_out=conf_w_out[0].astype(BF16),
        mem_w_k=mem_w_k[0].astype(BF16), mem_w_v=mem_w_v[0].astype(BF16), mem_w_o=mem_w_o[0].astype(BF16),
        w_out=w_out[0].astype(BF16), ln1_g=row(ln1_g), ln1_b=row(ln1_b),
        peer_w_q=peer_w_q[0].astype(BF16),
        peer_keys=_block_diag_keys(peer_sub_keys[0]).astype(BF16),
        peer_u=peer_u[0].astype(BF16), peer_v=peer_v[0].astype(BF16),
        ln2_g=row(ln2_g), ln2_b=row(ln2_b),
    )


def _tokenwise_tail(p, x, proj, ynorm, cact, attn, tm_merge, tm_score, tm_expert):
    x1 = _merge(ynorm, cact, attn, proj, x, p["ssd_w_out"], p["conf_w_out"], p["mem_w_o"], p["w_out"],
                p["ln1_g"], p["ln1_b"], tm_merge)
    npass, e0, rank, e1 = _peer_scores(x1, p["peer_w_q"], p["peer_keys"], tm_score)
    return _peer_experts(x1, p["peer_u"], p["peer_v"], npass, e0, rank, e1, p["ln2_g"], p["ln2_b"], tm_expert)


def _prompt_layer(p, x_prompt, mem_prompt):
    nb, seq, _ = x_prompt.shape
    x = x_prompt.reshape(nb * seq, D_MODEL)
    proj = _matmul(x, p["w_main"], 2048, 1024, "in_proj_prompt")
    dt_raw = _matmul(x, p["w_dt"], 1024, LANES, "dt_proj_prompt")
    mem = mem_prompt.reshape(nb * MEM_TOKENS, D_MODEL)
    mk = _matmul(mem, p["mem_w_k"], 1024, 1024, "mem_k_proj")
    mv = _matmul(mem, p["mem_w_v"], 1024, 1024, "mem_v_proj")
    xbc_c = _ssd_conv_prompt(proj, p["ssd_conv_w"], p["ssd_conv_b"], nb, seq)
    ynorm, ssm = _ssd_scan_prompt(xbc_c, proj, dt_raw, p["dt_bias"], p["a_log"], p["d_row"], p["norm_w"], nb, seq)
    cact, conf_state = _conf_prompt(proj, p["conf_w"], p["conf_b"], p["conf_ln_g"], p["conf_ln_b"], nb, seq)
    attn = _attn_prompt(proj, mk, mv, nb, seq)
    y = _tokenwise_tail(p, x, proj, ynorm, cact, attn, 256, 256, 512)
    ssd_buf = proj.reshape(nb, seq, PROJ_COLS)[:, seq - (SSD_CONV - 1):, S_Z:S_XBC]
    kv_shape = (nb, MEM_TOKENS, MEM_HEADS, MEM_HEADDIM)
    ssm = ssm.reshape(nb, SSD_HEADS, SSD_HEADDIM, SSD_STATE)
    return (y.reshape(nb, seq, D_MODEL), ssm, ssd_buf, conf_state, mk.reshape(kv_shape), mv.reshape(kv_shape))


def _sample_layer(p, x_sample, state_ssd, state_ssd_conv, state_conf_conv, cache_mem_k, cache_mem_v):
    nb = x_sample.shape[0]
    x = x_sample.reshape(nb, D_MODEL)
    proj = _matmul(x, p["w_main"], nb, 1024, "in_proj_sample")
    dt_raw = _matmul(x, p["w_dt"], nb, LANES, "dt_proj_sample")
    xbc_c = _ssd_conv_step(proj, jnp.swapaxes(state_ssd_conv, 0, 1), p["ssd_conv_w"], p["ssd_conv_b"], nb)
    rows = SSD_HEADS * SSD_HEADDIM
    ssm, y_t = _ssd_step(xbc_c, dt_raw, p["dt_bias"], p["a_log"], state_ssd.reshape(nb, rows, SSD_STATE), nb)
    ynorm = _gated_norm_step(y_t, xbc_c, proj, p["d_row"], p["norm_w"], nb)
    cact, glu = _conf_step(proj, jnp.swapaxes(state_conf_conv, 0, 1), p["conf_w"], p["conf_b"],
                           p["conf_ln_g"], p["conf_ln_b"], nb)
    attn = _attn_step(proj, cache_mem_k, cache_mem_v, nb)
    y = _tokenwise_tail(p, x, proj, ynorm, cact, attn, nb, nb, nb)
    ssd_buf = jnp.concatenate([state_ssd_conv[:, 1:], proj[:, None, S_Z:S_XBC]], axis=1)
    conf_buf = jnp.concatenate([state_conf_conv[:, 1:], glu[:, None, :]], axis=1)
    return (y.reshape(nb, 1, D_MODEL), ssm.reshape(nb, SSD_HEADS, SSD_HEADDIM, SSD_STATE), ssd_buf, conf_buf)


def kernel(x_prompt, x_sample, state_ssd, state_ssd_conv, state_conf_conv, cache_mem_k, cache_mem_v, mem_prompt, w_in, ssd_conv_w, ssd_conv_b, ssd_dt_bias, ssd_a_log, ssd_d, ssd_norm_w, ssd_w_out, conf_conv_w, conf_conv_b, conf_ln_g, conf_ln_b, conf_w_out, mem_w_k, mem_w_v, mem_w_o, w_out, ln1_g, ln1_b, peer_w_q, peer_sub_keys, peer_u, peer_v, ln2_g, ln2_b):
    assert w_in.shape[0] == DEPTH == 1
    p = _prepare_weights(w_in, ssd_conv_w, ssd_conv_b, ssd_dt_bias, ssd_a_log, ssd_d, ssd_norm_w, ssd_w_out,
                         conf_conv_w, conf_conv_b, conf_ln_g, conf_ln_b, conf_w_out, mem_w_k, mem_w_v, mem_w_o,
                         w_out, ln1_g, ln1_b, peer_w_q, peer_sub_keys, peer_u, peer_v, ln2_g, ln2_b)
    yp, ssm_p, sbuf_p, cbuf_p, mk_p, mv_p = _prompt_layer(p, x_prompt, mem_prompt)
    ys, ssm_s, sbuf_s, cbuf_s = _sample_layer(p, x_sample, state_ssd[0], state_ssd_conv[0], state_conf_conv[0],
                                              cache_mem_k[0], cache_mem_v[0])
    return (yp, ys, ssm_p[None], sbuf_p[None], cbuf_p[None], mk_p[None], mv_p[None],
            ssm_s[None], sbuf_s[None], cbuf_s[None])
```

```python
import functools
import math

import jax
import jax.numpy as jnp
from jax import lax
from jax.experimental import pallas as pl
from jax.experimental.pallas import tpu as pltpu

F32 = jnp.float32
BF16 = jnp.bfloat16

D_MODEL = 1024
SSD_INNER = 2048
SSD_HEADDIM = 64
SSD_HEADS = 32
SSD_GROUPS = 4
SSD_STATE = 128
SSD_CONV = 4
SSD_CHUNK = 128
SSD_XBC = 3072
CONF_DIM = 1024
CONF_WIDTH = 31
MEM_TOKENS = 256
MEM_HEADS = 4
MEM_HEADDIM = 256
PEER_HEADS = 8
PEER_NKEYS = 128
PEER_EXPERTS = PEER_NKEYS * PEER_NKEYS
PEER_DHALF = 128
PEER_TOPK = 16
DEPTH = 1
ALPHA = (2.0 * DEPTH) ** 0.25
LN_EPS = 1e-5
S_Z = SSD_INNER
S_XBC = S_Z + SSD_XBC
S_DT = S_XBC + SSD_HEADS

COL_XBC = 2
COL_CONF_A = 5
COL_CONF_B = 6
COL_QMEM = 7
COL_GATES = 8
PROJ_COLS = 11 * 1024

LANES = 128
SUBLANES = 8
VMEM_LIMIT_BYTES = 56 * 1024 * 1024


def _params(*sem):
    return pltpu.CompilerParams(dimension_semantics=sem, vmem_limit_bytes=VMEM_LIMIT_BYTES)


def _sigmoid(x):
    return jax.nn.sigmoid(x)


def _silu(x):
    return x * _sigmoid(x)


def _twice_gelu_erf(x):
    return x * (1.0 + lax.erf(x * (1.0 / math.sqrt(2.0))))


def _layer_norm_rows(x, g, b):
    mu = jnp.mean(x, axis=-1, keepdims=True)
    xc = x - mu
    var = jnp.mean(xc * xc, axis=-1, keepdims=True)
    return xc * lax.rsqrt(var + LN_EPS) * g + b


def _split3_bf16(x):
    hi = x.astype(BF16)
    r1 = x - hi.astype(F32)
    mid = r1.astype(BF16)
    lo = (r1 - mid.astype(F32)).astype(BF16)
    return hi, mid, lo


def _dot(a, b):
    return jnp.dot(a, b, preferred_element_type=F32)


def _dot_nt(a, b):
    return lax.dot_general(a, b, (((1,), (1,)), ((), ())), preferred_element_type=F32)


def _mm_kernel(x_ref, w_ref, o_ref):
    o_ref[...] = _dot(x_ref[...].astype(BF16), w_ref[...]).astype(o_ref.dtype)


def _matmul(x, w, tm, tn, name):
    m, k = x.shape
    n = w.shape[1]
    tm = min(tm, m)
    assert m % tm == 0 and n % tn == 0
    return pl.pallas_call(
        _mm_kernel,
        grid=(m // tm, n // tn),
        in_specs=[pl.BlockSpec((tm, k), lambda i, j: (i, 0)),
                  pl.BlockSpec((k, tn), lambda i, j: (0, j))],
        out_specs=pl.BlockSpec((tm, tn), lambda i, j: (i, j)),
        out_shape=jax.ShapeDtypeStruct((m, n), F32),
        compiler_params=_params("parallel", "parallel"),
        name=name,
    )(x, w)


SSD_CONV_ROWS = 512
CONV_CHUNK = 64


def _causal_conv_rows(src, r0, cols, w_ref, bias_ref, taps, first_off):
    win = -(-(first_off + taps - 1 + CONV_CHUNK) // SUBLANES) * SUBLANES
    window = src[r0:r0 + win, cols]
    acc = jnp.broadcast_to(bias_ref[:, cols], (CONV_CHUNK, LANES))
    for rho in range(SUBLANES):
        ks = [k for k in range(taps) if (first_off + k) % SUBLANES == rho]
        if not ks:
            continue
        shifted = window if rho == 0 else pltpu.roll(window, win - rho, 0)
        for k in ks:
            a = (first_off + k) // SUBLANES * SUBLANES
            acc = acc + w_ref[k:k + 1, cols] * shifted[a:a + CONV_CHUNK, :]
    return acc


def _ssd_conv_kernel(x_ref, halo_ref, w_ref, b_ref, o_ref, scr):
    tl = x_ref.shape[0]
    first = pl.program_id(1) == 0
    scr[0:SUBLANES, :] = jnp.where(first, 0.0, halo_ref[...])
    scr[SUBLANES:SUBLANES + tl, :] = x_ref[...]
    for r0 in range(0, tl, CONV_CHUNK):
        for c0 in range(0, x_ref.shape[1], LANES):
            cols = slice(c0, c0 + LANES)
            acc = _causal_conv_rows(scr, r0, cols, w_ref, b_ref, SSD_CONV, SUBLANES - (SSD_CONV - 1))
            o_ref[r0:r0 + CONV_CHUNK, cols] = _silu(acc)


def _ssd_conv_prompt(proj, w, b, nb, seq):
    tl = SSD_CONV_ROWS
    nl = seq // tl
    rows_per_halo = tl // SUBLANES
    return pl.pallas_call(
        _ssd_conv_kernel,
        grid=(nb, nl, SSD_XBC // 1024),
        in_specs=[
            pl.BlockSpec((tl, 1024), lambda bi, l, j: (bi * nl + l, COL_XBC + j)),
            pl.BlockSpec((SUBLANES, 1024),
                         lambda bi, l, j: (jnp.maximum((bi * nl + l) * rows_per_halo - 1, 0), COL_XBC + j)),
            pl.BlockSpec((SSD_CONV, 1024), lambda bi, l, j: (0, j)),
            pl.BlockSpec((1, 1024), lambda bi, l, j: (0, j)),
        ],
        out_specs=pl.BlockSpec((tl, 1024), lambda bi, l, j: (bi * nl + l, j)),
        out_shape=jax.ShapeDtypeStruct((nb * seq, SSD_XBC), F32),
        scratch_shapes=[pltpu.VMEM((SUBLANES + tl, 1024), F32)],
        compiler_params=_params("parallel", "parallel", "parallel"),
        name="ssd_conv_prompt",
    )(proj, proj, w, b)


def _softplus(x):
    return jnp.maximum(x, 0.0) + jnp.log1p(jnp.exp(-jnp.abs(x)))


def _ssd_scan_kernel(xc_ref, z_ref, dtraw_ref, dtb_ref, alog_ref, d_ref, nw_ref,
                     y_ref, st_ref, yscr):
    q = SSD_CHUNK
    c = pl.program_id(1)

    @pl.when(c == 0)
    def _():
        st_ref[...] = jnp.zeros_like(st_ref)

    dt = _softplus(dtraw_ref[...] + dtb_ref[...])
    a = -jnp.exp(alog_ref[...])
    da = dt * a
    row = lax.broadcasted_iota(jnp.int32, (q, q), 0)
    col = lax.broadcasted_iota(jnp.int32, (q, q), 1)
    causal = row >= col
    tri = jnp.where(causal, 1.0, 0.0).astype(BF16)
    hi, mid, lo = _split3_bf16(da)
    a_cs = _dot(tri, hi) + _dot(tri, mid) + _dot(tri, lo)
    a_cs_t = a_cs.T
    dt_t = dt.T
    xs = xc_ref[:, 0:SSD_INNER]
    xs_t = xs.T
    heads_per_group = SSD_HEADS // SSD_GROUPS
    rows_per_group = heads_per_group * SSD_HEADDIM
    first_half = lax.broadcasted_iota(jnp.int32, (q, 2 * SSD_HEADDIM), 1) < SSD_HEADDIM
    per_head_rows = lambda vals: jnp.concatenate(
        [jnp.broadcast_to(v, (SSD_HEADDIM, v.shape[1])) for v in vals], axis=0)
    for g in range(SSD_GROUPS):
        heads = range(g * heads_per_group, (g + 1) * heads_per_group)
        rows = slice(g * rows_per_group, (g + 1) * rows_per_group)
        b_g16 = xc_ref[:, SSD_INNER + g * SSD_STATE:SSD_INNER + (g + 1) * SSD_STATE].astype(BF16)
        c_g16 = xc_ref[:, SSD_INNER + (SSD_GROUPS + g) * SSD_STATE:
                       SSD_INNER + (SSD_GROUPS + g + 1) * SSD_STATE].astype(BF16)
        cb = _dot_nt(c_g16, b_g16)
        st_g = st_ref[0, rows, :]
        y_off = _dot_nt(c_g16, st_g.astype(BF16))
        last = [a_cs_t[h:h + 1, q - 1:q] for h in heads]
        w_rows = per_head_rows([jnp.exp(last[i] - a_cs_t[h:h + 1, :]) * dt_t[h:h + 1, :]
                                for i, h in enumerate(heads)])
        states = _dot((xs_t[rows, :] * w_rows).astype(BF16), b_g16)
        st_ref[0, rows, :] = st_g * per_head_rows([jnp.exp(v) * jnp.ones((1, SSD_STATE), F32) for v in last]) + states
        for r in range(0, heads_per_group, 2):
            h0 = g * heads_per_group + r
            m1, ea = [], []
            for h in (h0, h0 + 1):
                acs_col = a_cs[:, h:h + 1]
                seg = jnp.where(causal, acs_col - a_cs_t[h:h + 1, :], -1e30)
                m1.append((cb * jnp.exp(seg) * dt_t[h:h + 1, :]).astype(BF16))
                ea.append(jnp.broadcast_to(jnp.exp(acs_col), (q, SSD_HEADDIM)))
            cols = slice(h0 * SSD_HEADDIM, (h0 + 2) * SSD_HEADDIM)
            x_pair = xs[:, cols]
            x_diag = jnp.concatenate([jnp.where(first_half, x_pair, 0.0), jnp.where(first_half, 0.0, x_pair)],
                                     axis=0).astype(BF16)
            off_cols = slice(r * SSD_HEADDIM, (r + 2) * SSD_HEADDIM)
            yscr[:, cols] = (_dot(jnp.concatenate(m1, axis=1), x_diag)
                             + y_off[:, off_cols] * jnp.concatenate(ea, axis=1))
    y = yscr[...] + xs * d_ref[...]
    hgate = y * _silu(z_ref[...])
    ms = jnp.mean(hgate * hgate, axis=-1, keepdims=True)
    y_ref[...] = (hgate * lax.rsqrt(ms + LN_EPS) * nw_ref[...]).astype(y_ref.dtype)


def _ssd_scan_prompt(xbc_c, proj, dt_raw, dt_bias, a_log, d_row, norm_w, nb, seq):
    q = SSD_CHUNK
    nc = seq // q
    return pl.pallas_call(
        _ssd_scan_kernel,
        grid=(nb, nc),
        in_specs=[
            pl.BlockSpec((q, SSD_XBC), lambda bi, c: (bi * nc + c, 0)),
            pl.BlockSpec((q, SSD_INNER), lambda bi, c: (bi * nc + c, 0)),
            pl.BlockSpec((q, LANES), lambda bi, c: (bi * nc + c, 0)),
            pl.BlockSpec((1, LANES), lambda bi, c: (0, 0)),
            pl.BlockSpec((1, LANES), lambda bi, c: (0, 0)),
            pl.BlockSpec((1, SSD_INNER), lambda bi, c: (0, 0)),
            pl.BlockSpec((1, SSD_INNER), lambda bi, c: (0, 0)),
        ],
        out_specs=[
            pl.BlockSpec((q, SSD_INNER), lambda bi, c: (bi * nc + c, 0)),
            pl.BlockSpec((1, SSD_HEADS * SSD_HEADDIM, SSD_STATE), lambda bi, c: (bi, 0, 0)),
        ],
        out_shape=[jax.ShapeDtypeStruct((nb * seq, SSD_INNER), BF16),
                   jax.ShapeDtypeStruct((nb, SSD_HEADS * SSD_HEADDIM, SSD_STATE), F32)],
        scratch_shapes=[pltpu.VMEM((q, SSD_INNER), F32)],
        compiler_params=_params("parallel", "arbitrary"),
        name="ssd_scan_prompt",
    )(xbc_c, proj, dt_raw, dt_bias, a_log, d_row, norm_w)


def _ssd_conv_step_kernel(x_ref, st_ref, w_ref, b_ref, o_ref):
    acc = b_ref[...] + w_ref[SSD_CONV - 1:SSD_CONV, :] * x_ref[...]
    for k in range(SSD_CONV - 1):
        acc = acc + w_ref[k:k + 1, :] * st_ref[k]
    o_ref[...] = _silu(acc)


def _ssd_conv_step(proj, state_t, w, b, nb):
    return pl.pallas_call(
        _ssd_conv_step_kernel,
        grid=(SSD_XBC // 1024,),
        in_specs=[
            pl.BlockSpec((nb, 1024), lambda j: (0, COL_XBC + j)),
            pl.BlockSpec((SSD_CONV - 1, nb, 1024), lambda j: (0, 0, j)),
            pl.BlockSpec((SSD_CONV, 1024), lambda j: (0, j)),
            pl.BlockSpec((1, 1024), lambda j: (0, j)),
        ],
        out_specs=pl.BlockSpec((nb, 1024), lambda j: (0, j)),
        out_shape=jax.ShapeDtypeStruct((nb, SSD_XBC), F32),
        compiler_params=_params("parallel"),
        name="ssd_conv_step",
    )(proj, state_t, w, b)


def _ssd_step_kernel(xc_ref, bc_ref, dtraw_ref, dtb_ref, alog_ref, st_ref, st_out_ref, yt_ref, xdt_scr, dec_scr):
    nb = xc_ref.shape[0]
    bi = pl.program_id(0)
    rows = SSD_HEADS * SSD_HEADDIM

    @pl.when(bi == 0)
    def _():
        dt = _softplus(dtraw_ref[...] + dtb_ref[...])
        a = -jnp.exp(alog_ref[...])
        dt_t = dt.T
        da_t = (dt * a).T
        expand = lambda t: jnp.concatenate(
            [jnp.broadcast_to(t[h:h + 1, :], (SSD_HEADDIM, nb)) for h in range(SSD_HEADS)], axis=0)
        xs_t = xc_ref[:, 0:SSD_INNER].T
        xdt_scr[...] = xs_t * expand(dt_t)
        dec_scr[...] = jnp.exp(expand(da_t))
        yt_ref[...] = jnp.zeros_like(yt_ref)

    lane_b = lax.broadcasted_iota(jnp.int32, (nb, LANES), 0)
    onehot = jnp.where(lane_b == bi, 1.0, 0.0).astype(BF16)

    def pick(ref):
        hi, mid, lo = _split3_bf16(ref[...])
        return _dot(hi, onehot) + _dot(mid, onehot) + _dot(lo, onehot)

    xdt_b = pick(xdt_scr)
    dec_b = pick(dec_scr)
    out_lane = lax.broadcasted_iota(jnp.int32, (rows // SSD_GROUPS, LANES), 1) == bi
    for g in range(SSD_GROUPS):
        r0 = g * (rows // SSD_GROUPS)
        r1 = r0 + rows // SSD_GROUPS
        b_row = bc_ref[0, :, g * SSD_STATE:(g + 1) * SSD_STATE]
        c_row = bc_ref[0, :, (SSD_GROUPS + g) * SSD_STATE:(SSD_GROUPS + g + 1) * SSD_STATE]
        h_new = st_ref[0, r0:r1, :] * dec_b[r0:r1, :] + xdt_b[r0:r1, :] * b_row
        st_out_ref[0, r0:r1, :] = h_new
        y_col = jnp.sum(h_new * c_row, axis=-1, keepdims=True)
        yt_ref[r0:r1, :] = jnp.where(out_lane, y_col, yt_ref[r0:r1, :])


def _ssd_step(xbc_c, dt_raw, dt_bias, a_log, state, nb):
    rows = SSD_HEADS * SSD_HEADDIM
    assert nb == LANES
    bc_rows = xbc_c[:, SSD_INNER:].reshape(nb, 1, SSD_XBC - SSD_INNER)
    return pl.pallas_call(
        _ssd_step_kernel,
        grid=(nb,),
        in_specs=[
            pl.BlockSpec((nb, SSD_XBC), lambda bi: (0, 0)),
            pl.BlockSpec((1, 1, bc_rows.shape[2]), lambda bi: (bi, 0, 0)),
            pl.BlockSpec((nb, LANES), lambda bi: (0, 0)),
            pl.BlockSpec((1, LANES), lambda bi: (0, 0)),
            pl.BlockSpec((1, LANES), lambda bi: (0, 0)),
            pl.BlockSpec((1, rows, SSD_STATE), lambda bi: (bi, 0, 0)),
        ],
        out_specs=[
            pl.BlockSpec((1, rows, SSD_STATE), lambda bi: (bi, 0, 0)),
            pl.BlockSpec((rows, nb), lambda bi: (0, 0)),
        ],
        out_shape=[jax.ShapeDtypeStruct((nb, rows, SSD_STATE), F32),
                   jax.ShapeDtypeStruct((rows, nb), F32)],
        scratch_shapes=[pltpu.VMEM((rows, nb), F32), pltpu.VMEM((rows, nb), F32)],
        compiler_params=_params("arbitrary"),
        name="ssd_step",
    )(xbc_c, bc_rows, dt_raw, dt_bias, a_log, state)


def _gated_norm_step_kernel(yt_ref, xc_ref, z_ref, d_ref, nw_ref, o_ref):
    y = yt_ref[...].T + xc_ref[:, 0:SSD_INNER] * d_ref[...]
    hgate = y * _silu(z_ref[...])
    ms = jnp.mean(hgate * hgate, axis=-1, keepdims=True)
    o_ref[...] = (hgate * lax.rsqrt(ms + LN_EPS) * nw_ref[...]).astype(o_ref.dtype)


def _gated_norm_step(y_t, xbc_c, proj, d_row, norm_w, nb):
    return pl.pallas_call(
        _gated_norm_step_kernel,
        grid=(1,),
        in_specs=[
            pl.BlockSpec((SSD_INNER, nb), lambda i: (0, 0)),
            pl.BlockSpec((nb, SSD_XBC), lambda i: (0, 0)),
            pl.BlockSpec((nb, SSD_INNER), lambda i: (0, 0)),
            pl.BlockSpec((1, SSD_INNER), lambda i: (0, 0)),
            pl.BlockSpec((1, SSD_INNER), lambda i: (0, 0)),
        ],
        out_specs=pl.BlockSpec((nb, SSD_INNER), lambda i: (0, 0)),
        out_shape=jax.ShapeDtypeStruct((nb, SSD_INNER), BF16),
        compiler_params=_params("arbitrary"),
        name="ssd_gated_norm_step",
    )(y_t, xbc_c, proj, d_row, norm_w)


CONF_ROWS = 256
CONF_PAD = 32
CONF_W_ROWS = 32


def _conf_kernel(a_ref, b_ref, w_ref, cb_ref, g_ref, beta_ref, o_ref, st_ref, gl, cv):
    tl = a_ref.shape[0]
    l = pl.program_id(1)

    @pl.when(l == 0)
    def _():
        gl[0:CONF_PAD, :] = jnp.zeros((CONF_PAD, CONF_DIM), F32)

    gl[CONF_PAD:CONF_PAD + tl, :] = a_ref[...] * _sigmoid(b_ref[...])
    for r0 in range(0, tl, CONV_CHUNK):
        for c0 in range(0, CONF_DIM, LANES):
            cols = slice(c0, c0 + LANES)
            cv[r0:r0 + CONV_CHUNK, cols] = _causal_conv_rows(gl, r0, cols, w_ref, cb_ref, CONF_WIDTH,
                                                             CONF_PAD - (CONF_WIDTH - 1))
    o_ref[...] = _silu(_layer_norm_rows(cv[...], g_ref[...], beta_ref[...])).astype(o_ref.dtype)

    @pl.when(l == pl.num_programs(1) - 1)
    def _():
        st_ref[0] = gl[CONF_PAD + tl - (CONF_WIDTH - 1):CONF_PAD + tl, :]

    gl[0:CONF_PAD, :] = gl[tl:tl + CONF_PAD, :]


def _conf_prompt(proj, w, cb, ln_g, ln_b, nb, seq):
    tl = CONF_ROWS
    nl = seq // tl
    vec = pl.BlockSpec((1, CONF_DIM), lambda bi, l: (0, 0))
    return pl.pallas_call(
        _conf_kernel,
        grid=(nb, nl),
        in_specs=[
            pl.BlockSpec((tl, 1024), lambda bi, l: (bi * nl + l, COL_CONF_A)),
            pl.BlockSpec((tl, 1024), lambda bi, l: (bi * nl + l, COL_CONF_B)),
            pl.BlockSpec((CONF_W_ROWS, CONF_DIM), lambda bi, l: (0, 0)),
            vec, vec, vec,
        ],
        out_specs=[
            pl.BlockSpec((tl, CONF_DIM), lambda bi, l: (bi * nl + l, 0)),
            pl.BlockSpec((1, CONF_WIDTH - 1, CONF_DIM), lambda bi, l: (bi, 0, 0)),
        ],
        out_shape=[jax.ShapeDtypeStruct((nb * seq, CONF_DIM), BF16),
                   jax.ShapeDtypeStruct((nb, CONF_WIDTH - 1, CONF_DIM), F32)],
        scratch_shapes=[pltpu.VMEM((CONF_PAD + tl, CONF_DIM), F32), pltpu.VMEM((tl, CONF_DIM), F32)],
        compiler_params=_params("parallel", "arbitrary"),
        name="conf_prompt",
    )(proj, proj, w, cb, ln_g, ln_b)


def _conf_step_kernel(a_ref, b_ref, st_ref, w_ref, cb_ref, g_ref, beta_ref, o_ref, glu_ref):
    glu = a_ref[...] * _sigmoid(b_ref[...])
    glu_ref[...] = glu
    acc = cb_ref[...] + w_ref[CONF_WIDTH - 1:CONF_WIDTH, :] * glu
    for k in range(CONF_WIDTH - 1):
        acc = acc + w_ref[k:k + 1, :] * st_ref[k]
    o_ref[...] = _silu(_layer_norm_rows(acc, g_ref[...], beta_ref[...])).astype(o_ref.dtype)


def _conf_step(proj, state_t, w, cb, ln_g, ln_b, nb):
    vec = pl.BlockSpec((1, CONF_DIM), lambda i: (0, 0))
    return pl.pallas_call(
        _conf_step_kernel,
        grid=(1,),
        in_specs=[
            pl.BlockSpec((nb, 1024), lambda i: (0, COL_CONF_A)),
            pl.BlockSpec((nb, 1024), lambda i: (0, COL_CONF_B)),
            pl.BlockSpec((CONF_WIDTH - 1, nb, CONF_DIM), lambda i: (0, 0, 0)),
            pl.BlockSpec((CONF_W_ROWS, CONF_DIM), lambda i: (0, 0)),
            vec, vec, vec,
        ],
        out_specs=[pl.BlockSpec((nb, CONF_DIM), lambda i: (0, 0)),
                   pl.BlockSpec((nb, CONF_DIM), lambda i: (0, 0))],
        out_shape=[jax.ShapeDtypeStruct((nb, CONF_DIM), BF16),
                   jax.ShapeDtypeStruct((nb, CONF_DIM), F32)],
        compiler_params=_params("arbitrary"),
        name="conf_step",
    )(proj, proj, state_t, w, cb, ln_g, ln_b)


ATTN_ROWS = 512
MEM_SCALE = MEM_HEADDIM ** -0.5


def _softmax_lanes(sc):
    mx = jnp.max(sc, axis=-1, keepdims=True)
    e = jnp.exp(sc - mx)
    return e / jnp.sum(e, axis=-1, keepdims=True)


def _attn_kernel(q_ref, k_ref, v_ref, o_ref):
    for h in range(MEM_HEADS):
        sl = slice(h * MEM_HEADDIM, (h + 1) * MEM_HEADDIM)
        sc = _dot_nt(q_ref[:, sl].astype(BF16), k_ref[:, sl].astype(BF16)) * MEM_SCALE
        pr = _softmax_lanes(sc).astype(BF16)
        o_ref[:, sl] = _dot(pr, v_ref[:, sl].astype(BF16)).astype(o_ref.dtype)


def _attn_prompt(proj, mk, mv, nb, seq):
    tq = ATTN_ROWS
    nq = seq // tq
    kv = pl.BlockSpec((MEM_TOKENS, 1024), lambda bi, i: (bi, 0))
    return pl.pallas_call(
        _attn_kernel,
        grid=(nb, nq),
        in_specs=[pl.BlockSpec((tq, 1024), lambda bi, i: (bi * nq + i, COL_QMEM)), kv, kv],
        out_specs=pl.BlockSpec((tq, 1024), lambda bi, i: (bi * nq + i, 0)),
        out_shape=jax.ShapeDtypeStruct((nb * seq, 1024), BF16),
        compiler_params=_params("parallel", "parallel"),
        name="attn_prompt",
    )(proj, mk, mv)


ATTN_STEP_BATCH = 2


def _attn_step_kernel(q_ref, k_ref, v_ref, o_ref):
    for i in range(ATTN_STEP_BATCH):
        q = q_ref[i]
        sc = jnp.sum(k_ref[i] * q[None], axis=-1, keepdims=True) * MEM_SCALE
        e = jnp.exp(sc - jnp.max(sc, axis=0, keepdims=True))
        pr = e / jnp.sum(e, axis=0, keepdims=True)
        o_ref[i] = jnp.sum(pr * v_ref[i], axis=0)


def _attn_step(proj, mem_k, mem_v, nb):
    bb = ATTN_STEP_BATCH
    q = proj[:, COL_QMEM * 1024:(COL_QMEM + 1) * 1024].reshape(nb, MEM_HEADS, MEM_HEADDIM)
    kv = pl.BlockSpec((bb, MEM_TOKENS, MEM_HEADS, MEM_HEADDIM), lambda i: (i, 0, 0, 0))
    qo = pl.BlockSpec((bb, MEM_HEADS, MEM_HEADDIM), lambda i: (i, 0, 0))
    out = pl.pallas_call(
        _attn_step_kernel,
        grid=(nb // bb,),
        in_specs=[qo, kv, kv],
        out_specs=qo,
        out_shape=jax.ShapeDtypeStruct((nb, MEM_HEADS, MEM_HEADDIM), F32),
        compiler_params=_params("parallel"),
        name="attn_step",
    )(q, mem_k, mem_v)
    return out.reshape(nb, MEM_HEADS * MEM_HEADDIM)


def _merge_kernel(yn_ref, ca_ref, at_ref, ga_ref, gb_ref, gc_ref, x_ref, wa_ref, wb_ref, wc_ref, wo_ref,
                  g_ref, b_ref, o_ref):
    br_a = _dot(yn_ref[...].astype(BF16), wa_ref[...])
    br_b = _dot(ca_ref[...].astype(BF16), wb_ref[...])
    br_c = _dot(at_ref[...].astype(BF16), wc_ref[...])
    merged = _sigmoid(ga_ref[...]) * br_a + _sigmoid(gb_ref[...]) * br_b + _sigmoid(gc_ref[...]) * br_c
    res = ALPHA * x_ref[...] + _dot(merged.astype(BF16), wo_ref[...])
    o_ref[...] = _layer_norm_rows(res, g_ref[...], b_ref[...])


def _merge(ynorm, cact, attn, proj, x, w_a, w_b, w_c, w_o, ln_g, ln_b, tm):
    t = x.shape[0]
    row = lambda w: pl.BlockSpec((tm, w), lambda i: (i, 0))
    gate = lambda j: pl.BlockSpec((tm, D_MODEL), lambda i: (i, COL_GATES + j))
    full = lambda a: pl.BlockSpec(a.shape, lambda i: (0, 0))
    return pl.pallas_call(
        _merge_kernel,
        grid=(t // tm,),
        in_specs=[row(SSD_INNER), row(CONF_DIM), row(1024), gate(0), gate(1), gate(2),
                  row(D_MODEL), full(w_a), full(w_b), full(w_c), full(w_o), full(ln_g), full(ln_b)],
        out_specs=row(D_MODEL),
        out_shape=jax.ShapeDtypeStruct((t, D_MODEL), F32),
        compiler_params=_params("parallel"),
        name="merge_ln1",
    )(ynorm, cact, attn, proj, proj, proj, x, w_a, w_b, w_c, w_o, ln_g, ln_b)


N_SUBKEYS = 2 * PEER_HEADS
STAIRCASE = [(i, j) for i in range(PEER_TOPK) for j in range(PEER_TOPK) if (i + 1) * (j + 1) <= PEER_TOPK]


def _compare_exchange(v, i, j):
    a, b = v[i], v[j]
    v[i] = jnp.maximum(a, b)
    v[j] = jnp.minimum(a, b)


def _bitonic_merge_desc(v):
    n = len(v)
    j = n // 2
    while j >= 1:
        for i in range(n):
            if i ^ j > i:
                _compare_exchange(v, i, i ^ j)
        j //= 2


def _bitonic_sort_desc(v):
    n = len(v)
    k = 2
    while k <= n:
        j = k // 2
        while j >= 1:
            for i in range(n):
                l = i ^ j
                if l > i:
                    if i & k == 0:
                        _compare_exchange(v, i, l)
                    else:
                        _compare_exchange(v, l, i)
            j //= 2
        k *= 2


def _top16_sorted(s_t):
    v = [s_t[r * SUBLANES:(r + 1) * SUBLANES, :] for r in range(PEER_NKEYS // SUBLANES)]
    _bitonic_sort_desc(v)
    for shift in (4, 2, 1):
        other = [pltpu.roll(x, shift, 0) for x in v]
        v = [jnp.maximum(v[i], other[PEER_TOPK - 1 - i]) for i in range(PEER_TOPK)]
        _bitonic_merge_desc(v)
    return v


def _prefix_length(test, vals):
    w = jnp.where
    c8 = test(vals[7])
    c4 = test(w(c8, vals[11], vals[3]))
    c2 = test(w(c8, w(c4, vals[13], vals[9]), w(c4, vals[5], vals[1])))
    c1 = test(w(c8, w(c4, w(c2, vals[14], vals[12]), w(c2, vals[10], vals[8])),
                w(c4, w(c2, vals[6], vals[4]), w(c2, vals[2], vals[0]))))
    c16 = test(vals[15])
    return w(c8, 8.0, 0.0) + w(c4, 4.0, 0.0) + w(c2, 2.0, 0.0) + w(c1, 1.0, 0.0) + w(c16, 1.0, 0.0)


def _pack_factor():
    return 4 // jnp.dtype(BF16).itemsize


def _pack_rows(x):
    return pltpu.bitcast(x.astype(BF16), jnp.uint32)


def _unpack_rows(w):
    return pltpu.bitcast(w, BF16)


def _peer_score_kernel(x_ref, wq_ref, keys_ref, npass_ref, e0_ref, rank_ref, e1_ref, qv_scr, top_scr, s1_scr):
    tm = x_ref.shape[0]
    qv_scr[...] = _dot(x_ref[...].astype(BF16), wq_ref[...]).astype(BF16)
    for hk in range(N_SUBKEYS):
        q_hk = qv_scr[:, hk * PEER_DHALF:(hk + 1) * PEER_DHALF]
        s_t = _dot_nt(keys_ref[hk], q_hk)
        if hk % 2 == 0:
            npass_ref[hk // 2] = s_t
        else:
            s1_scr[hk // 2] = s_t
        top = _top16_sorted(s_t)
        for i in range(PEER_TOPK):
            top_scr[hk % 2, i, hk // 2:hk // 2 + 1, :] = top[i][0:1, :]
    a = [top_scr[0, i] for i in range(PEER_TOPK)]
    b = [top_scr[1, i] for i in range(PEER_TOPK)]
    cand = [a[i] + b[j] for i, j in STAIRCASE]
    ordered = cand + [jnp.full((PEER_HEADS, tm), -jnp.inf, F32)] * (64 - len(cand))
    _bitonic_sort_desc(ordered)
    tau = ordered[PEER_TOPK - 1]
    top_sum = a[0] + b[0]
    z = jnp.zeros((PEER_HEADS, tm), F32)
    for c in cand:
        z = z + jnp.where(c >= tau, jnp.exp(c - top_sum), 0.0)
    half_inv_z = 0.5 / z
    for h in range(PEER_HEADS):
        s0 = npass_ref[h]
        s1 = s1_scr[h]
        e0_ref[h] = jnp.exp(s0 - a[0][h:h + 1, :])
        e1_ref[h] = _pack_rows(jnp.exp(s1 - b[0][h:h + 1, :]) * half_inv_z[h:h + 1, :])
        b_rows = [b[r][h:h + 1, :] for r in range(PEER_TOPK)]
        tau_h = tau[h:h + 1, :]
        npass_ref[h] = _prefix_length(lambda v: s0 + v >= tau_h, b_rows)
        rank_ref[h] = _pack_rows(_prefix_length(lambda v: v > s1, b_rows))


def _peer_scores(x, w_q, keys, tm):
    t = x.shape[0]
    spec = pl.BlockSpec((PEER_HEADS, PEER_NKEYS, tm), lambda i: (0, 0, i))
    spec16 = pl.BlockSpec((PEER_HEADS, PEER_NKEYS // _pack_factor(), tm), lambda i: (0, 0, i))
    shape = jax.ShapeDtypeStruct((PEER_HEADS, PEER_NKEYS, t), F32)
    shape16 = jax.ShapeDtypeStruct((PEER_HEADS, PEER_NKEYS // _pack_factor(), t), jnp.uint32)
    return pl.pallas_call(
        _peer_score_kernel,
        grid=(t // tm,),
        in_specs=[pl.BlockSpec((tm, D_MODEL), lambda i: (i, 0)),
                  pl.BlockSpec(w_q.shape, lambda i: (0, 0)),
                  pl.BlockSpec(keys.shape, lambda i: (0, 0, 0))],
        out_specs=[spec, spec, spec16, spec16],
        out_shape=[shape, shape, shape16, shape16],
        scratch_shapes=[pltpu.VMEM((tm, N_SUBKEYS * PEER_DHALF), BF16),
                        pltpu.VMEM((2, PEER_TOPK, PEER_HEADS, tm), F32),
                        pltpu.VMEM((PEER_HEADS, PEER_NKEYS, tm), F32)],
        compiler_params=_params("parallel"),
        name="peer_scores",
    )(x, w_q, keys)


PEER_UNIT = 1024
I_PER_UNIT = PEER_UNIT // PEER_NKEYS
assert I_PER_UNIT == SUBLANES
UNITS = PEER_EXPERTS // PEER_UNIT
UNITS_PER_STEP = 2
II_GROUP = 2


def _peer_gate_chunk(act_ref, h_ref, npass_ref, e0_ref, rank_ref, e1_ref, c0, ii0):
    gate_sub = SUBLANES * _pack_factor()
    cols = slice(c0, c0 + LANES)
    sub = (gate_sub, LANES)
    zero = jnp.zeros(sub, BF16)
    row = lambda ref, h, ii: jnp.broadcast_to(ref[h, ii:ii + 1, cols], sub).astype(BF16)
    npass = [[row(npass_ref, h, ii0 + q) for h in range(PEER_HEADS)] for q in range(II_GROUP)]
    e0 = [[row(e0_ref, h, ii0 + q) for h in range(PEER_HEADS)] for q in range(II_GROUP)]
    for j0 in range(0, PEER_NKEYS, gate_sub):
        words = slice(j0 // _pack_factor(), j0 // _pack_factor() + SUBLANES)
        gate = [zero] * II_GROUP
        for h in range(PEER_HEADS):
            rank = _unpack_rows(rank_ref[h, words, cols])
            e1 = _unpack_rows(e1_ref[h, words, cols])
            for q in range(II_GROUP):
                gate[q] = gate[q] + e0[q][h] * jnp.where(rank < npass[q][h], e1, zero)
        for q in range(II_GROUP):
            r0 = (ii0 + q) * PEER_NKEYS + j0
            h_ref[r0:r0 + gate_sub, cols] = gate[q] * _twice_gelu_erf(act_ref[r0:r0 + gate_sub, cols]).astype(BF16)


def _peer_unit(tm, u_ref, vt_ref, off, npass_ref, e0_ref, rank_ref, e1_ref, xb, act, h, acc_t):
    act[...] = _dot_nt(u_ref[off:off + PEER_UNIT, :], xb[...])
    for c0 in range(0, tm, LANES):
        for ii in range(0, I_PER_UNIT, II_GROUP):
            _peer_gate_chunk(act, h, npass_ref, e0_ref, rank_ref, e1_ref, c0, ii)
    acc_t[...] += _dot(vt_ref[:, off:off + PEER_UNIT], h[...])


def _peer_expert_kernel(x_ref, u_ref, vt_ref, np_a, e0_a, np_b, e0_b, rank_ref, e1_ref, g_ref, b_ref, o_ref,
                        acc_t, act0, act1, h0, h1, xb):
    tm = x_ref.shape[0]
    step = pl.program_id(1)

    @pl.when(step == 0)
    def _():
        acc_t[...] = jnp.zeros_like(acc_t)
        xb[...] = x_ref[...].astype(BF16)

    _peer_unit(tm, u_ref, vt_ref, 0, np_a, e0_a, rank_ref, e1_ref, xb, act0, h0, acc_t)
    _peer_unit(tm, u_ref, vt_ref, PEER_UNIT, np_b, e0_b, rank_ref, e1_ref, xb, act1, h1, acc_t)

    @pl.when(step == pl.num_programs(1) - 1)
    def _():
        res = ALPHA * x_ref[...] + acc_t[...].T
        o_ref[...] = _layer_norm_rows(res, g_ref[...], b_ref[...])


def _peer_experts(x, u16, vt16, npass, e0, rank, e1, ln_g, ln_b, tm):
    t = x.shape[0]
    te = PEER_UNIT * UNITS_PER_STEP
    group = lambda q: pl.BlockSpec((PEER_HEADS, SUBLANES, tm), lambda i, s: (0, UNITS_PER_STEP * s + q, i))
    full = pl.BlockSpec((PEER_HEADS, PEER_NKEYS // _pack_factor(), tm), lambda i, s: (0, 0, i))
    vec = pl.BlockSpec((1, D_MODEL), lambda i, s: (0, 0))
    return pl.pallas_call(
        _peer_expert_kernel,
        grid=(t // tm, UNITS // UNITS_PER_STEP),
        in_specs=[pl.BlockSpec((tm, D_MODEL), lambda i, s: (i, 0)),
                  pl.BlockSpec((te, D_MODEL), lambda i, s: (s, 0)),
                  pl.BlockSpec((D_MODEL, te), lambda i, s: (0, s)),
                  group(0), group(0), group(1), group(1), full, full, vec, vec],
        out_specs=pl.BlockSpec((tm, D_MODEL), lambda i, s: (i, 0)),
        out_shape=jax.ShapeDtypeStruct((t, D_MODEL), F32),
        scratch_shapes=[pltpu.VMEM((D_MODEL, tm), F32),
                        pltpu.VMEM((PEER_UNIT, tm), F32), pltpu.VMEM((PEER_UNIT, tm), F32),
                        pltpu.VMEM((PEER_UNIT, tm), BF16), pltpu.VMEM((PEER_UNIT, tm), BF16),
                        pltpu.VMEM((tm, D_MODEL), BF16)],
        compiler_params=_params("parallel", "arbitrary"),
        name="peer_experts",
    )(x, u16, vt16, npass, e0, npass, e0, rank, e1, ln_g, ln_b)


def _pad_lanes(v):
    return jnp.pad(v, (0, LANES - v.shape[0])).reshape(1, LANES)


def _prepare_weights(w_in, ssd_conv_w, ssd_conv_b, ssd_dt_bias, ssd_a_log, ssd_d, ssd_norm_w, ssd_w_out,
                     conf_conv_w, conf_conv_b, conf_ln_g, conf_ln_b, conf_w_out, mem_w_k, mem_w_v, mem_w_o,
                     w_out, ln1_g, ln1_b, peer_w_q, peer_sub_keys, peer_u, peer_v, ln2_g, ln2_b):
    w = w_in[0]
    row = lambda v: v[0].reshape(1, -1)
    return dict(
        w_main=jnp.concatenate([w[:, :S_XBC], w[:, S_DT:]], axis=1).astype(BF16),
        w_dt=jnp.pad(w[:, S_XBC:S_DT], ((0, 0), (0, LANES - SSD_HEADS))).astype(BF16),
        ssd_conv_w=ssd_conv_w[0], ssd_conv_b=row(ssd_conv_b),
        dt_bias=_pad_lanes(ssd_dt_bias[0]), a_log=_pad_lanes(ssd_a_log[0]),
        d_row=jnp.repeat(ssd_d[0], SSD_HEADDIM).reshape(1, SSD_INNER), norm_w=row(ssd_norm_w),
        ssd_w_out=ssd_w_out[0].astype(BF16),
        conf_w=jnp.pad(conf_conv_w[0], ((0, CONF_W_ROWS - CONF_WIDTH), (0, 0))), conf_b=row(conf_conv_b),
        conf_ln_g=row(conf_ln_g), conf_ln_b=row(conf_ln_b), conf_w_out=conf_w_out[0].astype(BF16),
        mem_w_k=mem_w_k[0].astype(BF16), mem_w_v=mem_w_v[0].astype(BF16), mem_w_o=mem_w_o[0].astype(BF16),
        w_out=w_out[0].astype(BF16), ln1_g=row(ln1_g), ln1_b=row(ln1_b),
        peer_w_q=peer_w_q[0].astype(BF16),
        peer_keys=peer_sub_keys[0].reshape(N_SUBKEYS, PEER_NKEYS, PEER_DHALF).astype(BF16),
        peer_u=peer_u[0].astype(BF16), peer_vt=peer_v[0].T.astype(BF16),
        ln2_g=row(ln2_g), ln2_b=row(ln2_b),
    )


def _tokenwise_tail(p, x, proj, ynorm, cact, attn, tm_merge, tm_score, tm_expert):
    x1 = _merge(ynorm, cact, attn, proj, x, p["ssd_w_out"], p["conf_w_out"], p["mem_w_o"], p["w_out"],
                p["ln1_g"], p["ln1_b"], tm_merge)
    npass, e0, rank, e1 = _peer_scores(x1, p["peer_w_q"], p["peer_keys"], tm_score)
    return _peer_experts(x1, p["peer_u"], p["peer_vt"], npass, e0, rank, e1, p["ln2_g"], p["ln2_b"], tm_expert)


def _prompt_layer(p, x_prompt, mem_prompt):
    nb, seq, _ = x_prompt.shape
    x = x_prompt.reshape(nb * seq, D_MODEL)
    proj = _matmul(x, p["w_main"], 2048, 1024, "in_proj_prompt")
    dt_raw = _matmul(x, p["w_dt"], 1024, LANES, "dt_proj_prompt")
    mem = mem_prompt.reshape(nb * MEM_TOKENS, D_MODEL)
    mk = _matmul(mem, p["mem_w_k"], 1024, 1024, "mem_k_proj")
    mv = _matmul(mem, p["mem_w_v"], 1024, 1024, "mem_v_proj")
    xbc_c = _ssd_conv_prompt(proj, p["ssd_conv_w"], p["ssd_conv_b"], nb, seq)
    ynorm, ssm = _ssd_scan_prompt(xbc_c, proj, dt_raw, p["dt_bias"], p["a_log"], p["d_row"], p["norm_w"], nb, seq)
    cact, conf_state = _conf_prompt(proj, p["conf_w"], p["conf_b"], p["conf_ln_g"], p["conf_ln_b"], nb, seq)
    attn = _attn_prompt(proj, mk, mv, nb, seq)
    y = _tokenwise_tail(p, x, proj, ynorm, cact, attn, 512, 128, 512)
    ssd_buf = proj.reshape(nb, seq, PROJ_COLS)[:, seq - (SSD_CONV - 1):, S_Z:S_XBC]
    kv_shape = (nb, MEM_TOKENS, MEM_HEADS, MEM_HEADDIM)
    ssm = ssm.reshape(nb, SSD_HEADS, SSD_HEADDIM, SSD_STATE)
    return (y.reshape(nb, seq, D_MODEL), ssm, ssd_buf, conf_state, mk.reshape(kv_shape), mv.reshape(kv_shape))


def _sample_layer(p, x_sample, state_ssd, state_ssd_conv, state_conf_conv, cache_mem_k, cache_mem_v):
    nb = x_sample.shape[0]
    x = x_sample.reshape(nb, D_MODEL)
    proj = _matmul(x, p["w_main"], nb, 1024, "in_proj_sample")
    dt_raw = _matmul(x, p["w_dt"], nb, LANES, "dt_proj_sample")
    xbc_c = _ssd_conv_step(proj, jnp.swapaxes(state_ssd_conv, 0, 1), p["ssd_conv_w"], p["ssd_conv_b"], nb)
    rows = SSD_HEADS * SSD_HEADDIM
    ssm, y_t = _ssd_step(xbc_c, dt_raw, p["dt_bias"], p["a_log"], state_ssd.reshape(nb, rows, SSD_STATE), nb)
    ynorm = _gated_norm_step(y_t, xbc_c, proj, p["d_row"], p["norm_w"], nb)
    cact, glu = _conf_step(proj, jnp.swapaxes(state_conf_conv, 0, 1), p["conf_w"], p["conf_b"],
                           p["conf_ln_g"], p["conf_ln_b"], nb)
    attn = _attn_step(proj, cache_mem_k, cache_mem_v, nb)
    y = _tokenwise_tail(p, x, proj, ynorm, cact, attn, nb, nb, nb)
    ssd_buf = jnp.concatenate([state_ssd_conv[:, 1:], proj[:, None, S_Z:S_XBC]], axis=1)
    conf_buf = jnp.concatenate([state_conf_conv[:, 1:], glu[:, None, :]], axis=1)
    return (y.reshape(nb, 1, D_MODEL), ssm.reshape(nb, SSD_HEADS, SSD_HEADDIM, SSD_STATE), ssd_buf, conf_buf)


def kernel(x_prompt, x_sample, state_ssd, state_ssd_conv, state_conf_conv, cache_mem_k, cache_mem_v, mem_prompt, w_in, ssd_conv_w, ssd_conv_b, ssd_dt_bias, ssd_a_log, ssd_d, ssd_norm_w, ssd_w_out, conf_conv_w, conf_conv_b, conf_ln_g, conf_ln_b, conf_w_out, mem_w_k, mem_w_v, mem_w_o, w_out, ln1_g, ln1_b, peer_w_q, peer_sub_keys, peer_u, peer_v, ln2_g, ln2_b):
    assert w_in.shape[0] == DEPTH == 1
    p = _prepare_weights(w_in, ssd_conv_w, ssd_conv_b, ssd_dt_bias, ssd_a_log, ssd_d, ssd_norm_w, ssd_w_out,
                         conf_conv_w, conf_conv_b, conf_ln_g, conf_ln_b, conf_w_out, mem_w_k, mem_w_v, mem_w_o,
                         w_out, ln1_g, ln1_b, peer_w_q, peer_sub_keys, peer_u, peer_v, ln2_g, ln2_b)
    yp, ssm_p, sbuf_p, cbuf_p, mk_p, mv_p = _prompt_layer(p, x_prompt, mem_prompt)
    ys, ssm_s, sbuf_s, cbuf_s = _sample_layer(p, x_sample, state_ssd[0], state_ssd_conv[0], state_conf_conv[0],
                                              cache_mem_k[0], cache_mem_v[0])
    return (yp, ys, ssm_p[None], sbuf_p[None], cbuf_p[None], mk_p[None], mv_p[None],
            ssm_s[None], sbuf_s[None], cbuf_s[None])
```
